```python
import math
import jax, jax.numpy as jnp
from jax import lax
import numpy as np

D_MODEL = 2048
BATCH = 4
SEQ = 2048
DEPTH = 4
DEC_BATCH = 128
DEC_SEQ = 4
PAST_LEN = 16384
PAGE_SIZE = 128

D_MIX = D_MODEL
D_GLA = D_MIX // 2
GLA_HEADS = 4
GLA_DK = D_GLA // 2 // GLA_HEADS
GLA_DV = D_GLA // GLA_HEADS
GLA_RANK = 16
GLA_TAU = 16.0
GLA_CHUNK = 64
D_CONV = D_MIX // 4
CONV_W = 3
D_CM = D_MIX // 4
CM_HEADS = 4
CM_HD = D_CM // CM_HEADS
CM_CHUNK = 128
D_FF = ((8 * D_MODEL + 3 * 256 - 1) // (3 * 256)) * 256
EPS = 1e-6
IN_SIZES = (GLA_HEADS * GLA_DK, GLA_HEADS * GLA_DK, D_GLA, D_GLA, GLA_RANK,
            D_CONV, D_CONV, D_CONV, D_CM, D_CM)
D_IN = sum(IN_SIZES)

kernel_name = "hybrid_gla_shortconv_chunkmlp_step"


def _split_idx():
    idx, acc = [], 0
    for s in IN_SIZES[:-1]:
        acc += s
        idx.append(acc)
    return idx


def rmsnorm(x, g):
    xf = x.astype(jnp.float32)
    y = xf * lax.rsqrt(jnp.mean(xf * xf, axis=-1, keepdims=True) + EPS)
    return (y * g.astype(jnp.float32)).astype(x.dtype)


def layernorm(x, g, b):
    xf = x.astype(jnp.float32)
    mu = jnp.mean(xf, axis=-1, keepdims=True)
    xc = xf - mu
    y = xc * lax.rsqrt(jnp.mean(xc * xc, axis=-1, keepdims=True) + EPS)
    return (y * g.astype(jnp.float32) + b.astype(jnp.float32)).astype(x.dtype)


def gla_chunked(q, k, v, loga, S0):
    B, L = q.shape[0], q.shape[1]
    C = math.gcd(L, GLA_CHUNK)
    N = L // C

    def blk(t):
        return t.reshape(B, N, C, GLA_HEADS, t.shape[-1]).transpose(0, 3, 1, 2, 4).astype(jnp.float32)

    qb, kb, vb, ab = blk(q), blk(k), blk(v), blk(loga)
    b = jnp.cumsum(ab, axis=3)
    g = b[:, :, :, -1]
    qt = qb * jnp.exp(b)
    kt = kb * jnp.exp(-b)
    kd = kb * jnp.exp(g[:, :, :, None] - b)
    mask = jnp.tril(jnp.ones((C, C), dtype=bool))
    A = jnp.where(mask, jnp.einsum('bhncd,bhnsd->bhncs', qt, kt), 0.0)
    o_intra = jnp.einsum('bhncs,bhnse->bhnce', A, vb)
    dS = jnp.einsum('bhncd,bhnce->bhnde', kd, vb)

    def step(S, inp):
        g_n, dS_n = inp
        return jnp.exp(g_n)[..., None] * S + dS_n, S

    S_fin, S_prev = lax.scan(step, S0.astype(jnp.float32),
                             (jnp.moveaxis(g, 2, 0), jnp.moveaxis(dS, 2, 0)))
    S_prev = jnp.moveaxis(S_prev, 0, 2)
    o = o_intra + jnp.einsum('bhncd,bhnde->bhnce', qt, S_prev)
    o = o.transpose(0, 2, 3, 1, 4).reshape(B, L, GLA_HEADS, GLA_DV)
    return o, S_fin.astype(S0.dtype)


def mixer(xn, S0, conv_buf, w_in, w_a2, b_a, gla_g, conv_w, conv_b, cm_ln_g, cm_ln_b, cm_ws, cm_bs, w_out):
    B, L, _ = xn.shape
    proj = xn @ w_in
    q, k, v, r, a_lr, cb, cc, ch, cu, cv = jnp.split(proj, _split_idx(), axis=-1)

    loga = jax.nn.log_sigmoid((a_lr @ w_a2 + b_a).astype(jnp.float32)) / GLA_TAU
    o, S_new = gla_chunked((q * (GLA_DK ** -0.5)).reshape(B, L, GLA_HEADS, GLA_DK),
                           k.reshape(B, L, GLA_HEADS, GLA_DK),
                           v.reshape(B, L, GLA_HEADS, GLA_DV),
                           loga.reshape(B, L, GLA_HEADS, GLA_DK), S0)
    o = o * lax.rsqrt(jnp.mean(o * o, axis=-1, keepdims=True) + EPS)
    o = o * gla_g.reshape(GLA_HEADS, GLA_DV).astype(jnp.float32)
    o_gla = o.reshape(B, L, D_GLA).astype(xn.dtype) * jax.nn.silu(r)

    z = cc * ch
    zp = jnp.concatenate([conv_buf.astype(z.dtype), z], axis=1)
    conv = conv_b
    for j in range(CONV_W):
        conv = conv + conv_w[j] * zp[:, j:j + L]
    o_conv = cb * conv
    conv_new = zp[:, L:]

    u = jax.nn.gelu(cu, approximate=False)
    vv = layernorm(jax.nn.gelu(cv, approximate=False), cm_ln_g, cm_ln_b)
    T = min(L, CM_CHUNK)
    N = L // T
    vb = vv.reshape(B, N, T, CM_HEADS, CM_HD)
    Wm = jnp.where(jnp.tril(jnp.ones((T, T), dtype=bool)), cm_ws[:, :T, :T], 0.0)
    zc = jnp.einsum('hts,bnshc->bnthc', Wm, vb) + jnp.transpose(cm_bs[:, :T])[None, None, :, :, None]
    o_cm = u * zc.reshape(B, L, D_CM)

    out = jnp.concatenate([o_gla, o_conv, o_cm], axis=-1) @ w_out
    return out, S_new, conv_new, vv


def swiglu(x, w_gate, w_up, w_down):
    return (jax.nn.silu(x @ w_gate) * (x @ w_up)) @ w_down


def setup_inputs(seed: int = 0) -> dict:
    key = jax.random.key(seed)
    ks = jax.random.split(key, 24)
    nrm = lambda k, s: jax.random.normal(k, s, dtype=jnp.float32)
    return {
        "x_prompt": nrm(ks[0], (BATCH, SEQ, D_MODEL)),
        "x_sample": nrm(ks[1], (DEC_BATCH, DEC_SEQ, D_MODEL)),
        "state_gla": nrm(ks[2], (DEPTH, DEC_BATCH, GLA_HEADS, GLA_DK, GLA_DV)),
        "state_conv": nrm(ks[3], (DEPTH, DEC_BATCH, CONV_W - 1, D_CONV)),
        "norm1_g": 1.0 + 0.05 * nrm(ks[4], (DEPTH, D_MODEL)),
        "w_in": nrm(ks[5], (DEPTH, D_MODEL, D_IN)) * D_MODEL ** -0.5,
        "w_a2": nrm(ks[6], (DEPTH, GLA_RANK, GLA_HEADS * GLA_DK)) * GLA_RANK ** -0.5,
        "b_a": 0.1 * nrm(ks[7], (DEPTH, GLA_HEADS * GLA_DK)),
        "gla_g": 1.0 + 0.05 * nrm(ks[8], (DEPTH, D_GLA)),
        "conv_w": nrm(ks[9], (DEPTH, CONV_W, D_CONV)) * CONV_W ** -0.5,
        "conv_b": 0.01 * nrm(ks[10], (DEPTH, D_CONV)),
        "cm_ln_g": 1.0 + 0.05 * nrm(ks[11], (DEPTH, D_CM)),
        "cm_ln_b": 0.01 * nrm(ks[12], (DEPTH, D_CM)),
        "cm_ws": 0.1 * nrm(ks[13], (DEPTH, CM_HEADS, CM_CHUNK, CM_CHUNK)),
        "cm_bs": 1.0 + 0.1 * nrm(ks[14], (DEPTH, CM_HEADS, CM_CHUNK)),
        "w_out": nrm(ks[15], (DEPTH, D_MIX, D_MODEL)) * D_MIX ** -0.5,
        "norm2_g": 1.0 + 0.05 * nrm(ks[16], (DEPTH, D_MODEL)),
        "w_gate": nrm(ks[17], (DEPTH, D_MODEL, D_FF)) * D_MODEL ** -0.5,
        "w_up": nrm(ks[18], (DEPTH, D_MODEL, D_FF)) * D_MODEL ** -0.5,
        "w_down": nrm(ks[19], (DEPTH, D_FF, D_MODEL)) * D_FF ** -0.5,
        "final_g": 1.0 + 0.05 * nrm(ks[20], (D_MODEL,)),
    }


def reference(x_prompt, x_sample, state_gla, state_conv, norm1_g, w_in, w_a2, b_a, gla_g, conv_w, conv_b,
              cm_ln_g, cm_ln_b, cm_ws, cm_bs, w_out, norm2_g, w_gate, w_up, w_down, final_g):
    xp, xs = x_prompt, x_sample
    S0p = jnp.zeros((BATCH, GLA_HEADS, GLA_DK, GLA_DV), dtype=x_prompt.dtype)
    cbp = jnp.zeros((BATCH, CONV_W - 1, D_CONV), dtype=x_prompt.dtype)
    gla_p, conv_p, gla_s, conv_s, cmv_s = [], [], [], [], []
    for l in range(DEPTH):
        lw = (w_in[l], w_a2[l], b_a[l], gla_g[l], conv_w[l], conv_b[l], cm_ln_g[l], cm_ln_b[l],
              cm_ws[l], cm_bs[l], w_out[l])
        hp, Sp, cp, _ = mixer(rmsnorm(xp, norm1_g[l]), S0p, cbp, *lw)
        xp = xp + hp
        xp = xp + swiglu(rmsnorm(xp, norm2_g[l]), w_gate[l], w_up[l], w_down[l])
        hs, Ss, cs, vs = mixer(rmsnorm(xs, norm1_g[l]), state_gla[l], state_conv[l], *lw)
        xs = xs + hs
        xs = xs + swiglu(rmsnorm(xs, norm2_g[l]), w_gate[l], w_up[l], w_down[l])
        gla_p.append(Sp); conv_p.append(cp)
        gla_s.append(Ss); conv_s.append(cs); cmv_s.append(vs)
    y_prompt = rmsnorm(xp, final_g)
    y_sample = rmsnorm(xs, final_g)
    return (y_prompt, y_sample, jnp.stack(gla_p), jnp.stack(conv_p), jnp.stack(gla_s), jnp.stack(conv_s), jnp.stack(cmv_s))
```

```python
import functools
import math

import jax
import jax.numpy as jnp
from jax import lax
from jax.experimental import pallas as pl
from jax.experimental.pallas import tpu as pltpu

D_MODEL = 2048
BATCH = 4
SEQ = 2048
DEPTH = 4
DEC_BATCH = 128
DEC_SEQ = 4
D_GLA = 1024
GLA_HEADS = 4
GLA_DK = 128
GLA_DV = 256
GLA_RANK = 16
GLA_TAU = 16.0
GLA_CHUNK = 64
D_CONV = 512
CONV_W = 3
D_CM = 512
CM_HEADS = 4
CM_HD = 128
CM_CHUNK = 128
D_FF = 5632
EPS = 1e-6

N_PROMPT = BATCH * SEQ
N_SAMPLE = DEC_BATCH * DEC_SEQ
N_TOK = N_PROMPT + N_SAMPLE

OFF_Q, OFF_K, OFF_V, OFF_R = 0, 512, 1024, 2048
OFF_CB, OFF_CC, OFF_CH, OFF_CU, OFF_CV = 3072, 3584, 4096, 4608, 5120
OFF_A = 5632
D_IN_PAD = 5760

VMEM_LIMIT_V7X = 56 * 1024 * 1024

TM = 1088
TM_OUT = 544
TN_IN = 1152
TN_FF = 512
TN_DOWN = 512
NORM_ROWS = 32


def _params(sem):
    return pltpu.CompilerParams(dimension_semantics=sem, vmem_limit_bytes=VMEM_LIMIT_V7X)


def _rms_rows(x_ref, g_ref, o_ref, rows):
    g = g_ref[...]

    def body(c, carry):
        sl = pl.ds(pl.multiple_of(c * NORM_ROWS, NORM_ROWS), NORM_ROWS)
        xf = x_ref[sl, :]
        ms = jnp.mean(xf * xf, axis=-1, keepdims=True)
        o_ref[sl, :] = (xf * lax.rsqrt(ms + EPS) * g).astype(o_ref.dtype)
        return carry

    lax.fori_loop(0, rows // NORM_ROWS, body, 0)


def _in_proj_kernel(x_ref, g_ref, w_ref, o_ref, xn_ref):
    @pl.when(pl.program_id(1) == 0)
    def _():
        _rms_rows(x_ref, g_ref, xn_ref, TM)

    o_ref[...] = jnp.dot(xn_ref[...], w_ref[...],
                         preferred_element_type=jnp.float32).astype(o_ref.dtype)


def in_proj(x, g, w):
    return pl.pallas_call(
        _in_proj_kernel,
        grid=(N_TOK // TM, D_IN_PAD // TN_IN),
        in_specs=[pl.BlockSpec((TM, D_MODEL), lambda i, j: (i, 0)),
                  pl.BlockSpec((1, D_MODEL), lambda i, j: (0, 0)),
                  pl.BlockSpec((D_MODEL, TN_IN), lambda i, j: (0, j))],
        out_specs=pl.BlockSpec((TM, TN_IN), lambda i, j: (i, j)),
        out_shape=jax.ShapeDtypeStruct((N_TOK, D_IN_PAD), jnp.bfloat16),
        scratch_shapes=[pltpu.VMEM((TM, D_MODEL), jnp.bfloat16)],
        compiler_params=_params(("arbitrary", "arbitrary")),
        name="in_proj",
    )(x, g, w)


def _out_proj_kernel(mix_ref, w_ref, x_ref, g_ref, h_ref, hn_ref):
    h_ref[...] = x_ref[...] + jnp.dot(mix_ref[...], w_ref[...],
                                      preferred_element_type=jnp.float32)
    _rms_rows(h_ref, g_ref, hn_ref, TM_OUT)


def out_proj(mix, w, x, g):
    return pl.pallas_call(
        _out_proj_kernel,
        grid=(N_TOK // TM_OUT,),
        in_specs=[pl.BlockSpec((TM_OUT, D_MODEL), lambda i: (i, 0)),
                  pl.BlockSpec((D_MODEL, D_MODEL), lambda i: (0, 0)),
                  pl.BlockSpec((TM_OUT, D_MODEL), lambda i: (i, 0)),
                  pl.BlockSpec((1, D_MODEL), lambda i: (0, 0))],
        out_specs=[pl.BlockSpec((TM_OUT, D_MODEL), lambda i: (i, 0)),
                   pl.BlockSpec((TM_OUT, D_MODEL), lambda i: (i, 0))],
        out_shape=[jax.ShapeDtypeStruct((N_TOK, D_MODEL), jnp.float32),
                   jax.ShapeDtypeStruct((N_TOK, D_MODEL), jnp.bfloat16)],
        compiler_params=_params(("arbitrary",)),
        name="out_proj",
    )(mix, w, x, g)


def _ffn_up_kernel(hn_ref, wg_ref, wu_ref, o_ref):
    hn = hn_ref[...]
    gate = jnp.dot(hn, wg_ref[...], preferred_element_type=jnp.float32)
    up = jnp.dot(hn, wu_ref[...], preferred_element_type=jnp.float32)
    o_ref[...] = (jax.nn.silu(gate) * up).astype(o_ref.dtype)


def ffn_up(hn, wg, wu):
    return pl.pallas_call(
        _ffn_up_kernel,
        grid=(N_TOK // TM, D_FF // TN_FF),
        in_specs=[pl.BlockSpec((TM, D_MODEL), lambda i, j: (i, 0)),
                  pl.BlockSpec((D_MODEL, TN_FF), lambda i, j: (0, j)),
                  pl.BlockSpec((D_MODEL, TN_FF), lambda i, j: (0, j))],
        out_specs=pl.BlockSpec((TM, TN_FF), lambda i, j: (i, j)),
        out_shape=jax.ShapeDtypeStruct((N_TOK, D_FF), jnp.bfloat16),
        compiler_params=_params(("arbitrary", "arbitrary")),
        name="ffn_up",
    )(hn, wg, wu)


def _ffn_down_kernel(a_ref, w_ref, h_ref, o_ref):
    o_ref[...] = h_ref[...] + jnp.dot(a_ref[...], w_ref[...],
                                      preferred_element_type=jnp.float32)


def ffn_down(act, w, h):
    return pl.pallas_call(
        _ffn_down_kernel,
        grid=(N_TOK // TM, D_MODEL // TN_DOWN),
        in_specs=[pl.BlockSpec((TM, D_FF), lambda i, j: (i, 0)),
                  pl.BlockSpec((D_FF, TN_DOWN), lambda i, j: (0, j)),
                  pl.BlockSpec((TM, TN_DOWN), lambda i, j: (i, j))],
        out_specs=pl.BlockSpec((TM, TN_DOWN), lambda i, j: (i, j)),
        out_shape=jax.ShapeDtypeStruct((N_TOK, D_MODEL), jnp.float32),
        compiler_params=_params(("arbitrary", "arbitrary")),
        name="ffn_down",
    )(act, w, h)


def _final_norm_kernel(x_ref, g_ref, o_ref):
    _rms_rows(x_ref, g_ref, o_ref, TM_OUT)


def final_norm(x, g):
    return pl.pallas_call(
        _final_norm_kernel,
        grid=(N_TOK // TM_OUT,),
        in_specs=[pl.BlockSpec((TM_OUT, D_MODEL), lambda i: (i, 0)),
                  pl.BlockSpec((1, D_MODEL), lambda i: (0, 0))],
        out_specs=pl.BlockSpec((TM_OUT, D_MODEL), lambda i: (i, 0)),
        out_shape=jax.ShapeDtypeStruct((N_TOK, D_MODEL), jnp.float32),
        compiler_params=_params(("arbitrary",)),
        name="final_norm",
    )(x, g)


def _layernorm(x, g, b):
    mu = jnp.mean(x, axis=-1, keepdims=True)
    xc = x - mu
    return xc * lax.rsqrt(jnp.mean(xc * xc, axis=-1, keepdims=True) + EPS) * g + b


def _gla_chunked(q, k, v, loga, S0):
    B, L = q.shape[0], q.shape[1]
    C = math.gcd(L, GLA_CHUNK)
    N = L // C

    def blk(t):
        return t.reshape(B, N, C, GLA_HEADS, t.shape[-1]).transpose(0, 3, 1, 2, 4)

    qb, kb, vb, ab = blk(q), blk(k), blk(v), blk(loga)
    b = jnp.cumsum(ab, axis=3)
    g = b[:, :, :, -1]
    qt = qb * jnp.exp(b)
    kt = kb * jnp.exp(-b)
    kd = kb * jnp.exp(g[:, :, :, None] - b)
    mask = jnp.tril(jnp.ones((C, C), dtype=bool))
    A = jnp.where(mask, jnp.einsum('bhncd,bhnsd->bhncs', qt, kt), 0.0)
    o_intra = jnp.einsum('bhncs,bhnse->bhnce', A, vb)
    dS = jnp.einsum('bhncd,bhnce->bhnde', kd, vb)

    def step(S, inp):
        g_n, dS_n = inp
        return jnp.exp(g_n)[..., None] * S + dS_n, S

    S_fin, S_prev = lax.scan(step, S0, (jnp.moveaxis(g, 2, 0), jnp.moveaxis(dS, 2, 0)))
    S_prev = jnp.moveaxis(S_prev, 0, 2)
    o = o_intra + jnp.einsum('bhncd,bhnde->bhnce', qt, S_prev)
    o = o.transpose(0, 2, 3, 1, 4).reshape(B, L, GLA_HEADS, GLA_DV)
    return o, S_fin


def _mixer(proj, S0, conv_buf, w_a2, b_a, gla_g, conv_w, conv_b, cm_ln_g, cm_ln_b, cm_ws, cm_bs):
    B, L, _ = proj.shape
    p = proj.astype(jnp.float32)
    q, k = p[..., OFF_Q:OFF_K], p[..., OFF_K:OFF_V]
    v, r = p[..., OFF_V:OFF_R], p[..., OFF_R:OFF_CB]
    cb, cc, ch = p[..., OFF_CB:OFF_CC], p[..., OFF_CC:OFF_CH], p[..., OFF_CH:OFF_CU]
    cu, cv = p[..., OFF_CU:OFF_CV], p[..., OFF_CV:OFF_A]
    a_lr = p[..., OFF_A:OFF_A + GLA_RANK]

    loga = jax.nn.log_sigmoid(a_lr @ w_a2 + b_a) / GLA_TAU
    o, S_new = _gla_chunked((q * (GLA_DK ** -0.5)).reshape(B, L, GLA_HEADS, GLA_DK),
                            k.reshape(B, L, GLA_HEADS, GLA_DK),
                            v.reshape(B, L, GLA_HEADS, GLA_DV),
                            loga.reshape(B, L, GLA_HEADS, GLA_DK), S0)
    o = o * lax.rsqrt(jnp.mean(o * o, axis=-1, keepdims=True) + EPS)
    o = o * gla_g.reshape(GLA_HEADS, GLA_DV)
    o_gla = o.reshape(B, L, D_GLA) * jax.nn.silu(r)

    z = cc * ch
    zp = jnp.concatenate([conv_buf, z], axis=1)
    conv = conv_b
    for j in range(CONV_W):
        conv = conv + conv_w[j] * zp[:, j:j + L]
    o_conv = cb * conv
    conv_new = zp[:, L:]

    u = jax.nn.gelu(cu, approximate=False)
    vv = _layernorm(jax.nn.gelu(cv, approximate=False), cm_ln_g, cm_ln_b)
    T = min(L, CM_CHUNK)
    N = L // T
    vb = vv.reshape(B, N, T, CM_HEADS, CM_HD)
    Wm = jnp.where(jnp.tril(jnp.ones((T, T), dtype=bool)), cm_ws[:, :T, :T], 0.0)
    zc = jnp.einsum('hts,bnshc->bnthc', Wm, vb) + jnp.transpose(cm_bs[:, :T])[None, None, :, :, None]
    o_cm = u * zc.reshape(B, L, D_CM)

    mix = jnp.concatenate([o_gla, o_conv, o_cm], axis=-1).astype(jnp.bfloat16)
    return mix, S_new, conv_new, vv


def _pack_w_in(w_in):
    a0 = OFF_CB
    a1 = a0 + GLA_RANK
    pad = jnp.zeros(w_in.shape[:2] + (D_IN_PAD - OFF_A - GLA_RANK,), w_in.dtype)
    return jnp.concatenate([w_in[..., :a0], w_in[..., a1:], w_in[..., a0:a1], pad],
                           axis=-1).astype(jnp.bfloat16)


def kernel(x_prompt, x_sample, state_gla, state_conv, norm1_g, w_in, w_a2, b_a, gla_g, conv_w, conv_b,
           cm_ln_g, cm_ln_b, cm_ws, cm_bs, w_out, norm2_g, w_gate, w_up, w_down, final_g):
    bf = jnp.bfloat16
    w_in_p = _pack_w_in(w_in)
    w_out_b, w_gate_b, w_up_b, w_down_b = (w.astype(bf) for w in (w_out, w_gate, w_up, w_down))

    x = jnp.concatenate([x_prompt.reshape(N_PROMPT, D_MODEL),
                         x_sample.reshape(N_SAMPLE, D_MODEL)], axis=0)
    S0p = jnp.zeros((BATCH, GLA_HEADS, GLA_DK, GLA_DV), jnp.float32)
    cbp = jnp.zeros((BATCH, CONV_W - 1, D_CONV), jnp.float32)
    gla_p, conv_p, gla_s, conv_s, cmv_s = [], [], [], [], []
    for l in range(DEPTH):
        lw = (w_a2[l], b_a[l], gla_g[l], conv_w[l], conv_b[l], cm_ln_g[l], cm_ln_b[l], cm_ws[l], cm_bs[l])
        proj = in_proj(x, norm1_g[l].reshape(1, D_MODEL), w_in_p[l])
        mix_p, Sp, cp, _ = _mixer(proj[:N_PROMPT].reshape(BATCH, SEQ, D_IN_PAD), S0p, cbp, *lw)
        mix_s, Ss, cs, vs = _mixer(proj[N_PROMPT:].reshape(DEC_BATCH, DEC_SEQ, D_IN_PAD),
                                   state_gla[l], state_conv[l], *lw)
        mix = jnp.concatenate([mix_p.reshape(N_PROMPT, D_MODEL), mix_s.reshape(N_SAMPLE, D_MODEL)], axis=0)
        h, hn = out_proj(mix, w_out_b[l], x, norm2_g[l].reshape(1, D_MODEL))
        act = ffn_up(hn, w_gate_b[l], w_up_b[l])
        x = ffn_down(act, w_down_b[l], h)
        gla_p.append(Sp); conv_p.append(cp)
        gla_s.append(Ss); conv_s.append(cs); cmv_s.append(vs)
    y = final_norm(x, final_g.reshape(1, D_MODEL))
    y_prompt = y[:N_PROMPT].reshape(BATCH, SEQ, D_MODEL)
    y_sample = y[N_PROMPT:].reshape(DEC_BATCH, DEC_SEQ, D_MODEL)
    return (y_prompt, y_sample, jnp.stack(gla_p), jnp.stack(conv_p), jnp.stack(gla_s),
            jnp.stack(conv_s), jnp.stack(cmv_s))
```

```python
import functools
import math

import jax
import jax.numpy as jnp
from jax import lax
from jax.experimental import pallas as pl
from jax.experimental.pallas import tpu as pltpu

D_MODEL = 2048
BATCH = 4
SEQ = 2048
DEPTH = 4
DEC_BATCH = 128
DEC_SEQ = 4
D_GLA = 1024
GLA_HEADS = 4
GLA_DK = 128
GLA_DV = 256
GLA_RANK = 16
GLA_TAU = 16.0
GLA_CHUNK = 64
D_CONV = 512
CONV_W = 3
D_CM = 512
CM_HEADS = 4
CM_HD = 128
CM_CHUNK = 128
D_FF = 5632
EPS = 1e-6
SQRT_HALF = math.sqrt(0.5)

N_PROMPT = BATCH * SEQ
N_SAMPLE = DEC_BATCH * DEC_SEQ
N_TOK = N_PROMPT + N_SAMPLE

OFF_Q, OFF_K, OFF_V, OFF_R = 0, 512, 1024, 2048
OFF_CB, OFF_CC, OFF_CH, OFF_CU, OFF_CV = 3072, 3584, 4096, 4608, 5120
OFF_A = 5632
A_PAD = 128
D_IN_PAD = OFF_A + A_PAD
MIX_CONV, MIX_CM = D_GLA, D_GLA + D_CONV

LANES_V7X = 128
SUBLANES_V7X = 8
VMEM_LIMIT_V7X = 56 * 1024 * 1024

TM = 1088
TM_OUT = 512
TN_IN = 1152
TN_FF = 512
TN_DOWN = 512
NORM_ROWS = 32
TT = 512


def _params(sem):
    return pltpu.CompilerParams(dimension_semantics=sem, vmem_limit_bytes=VMEM_LIMIT_V7X)


def _rms_rows(x_ref, g_ref, o_ref, rows):
    g = g_ref[...]

    def body(c, carry):
        sl = pl.ds(pl.multiple_of(c * NORM_ROWS, NORM_ROWS), NORM_ROWS)
        xf = x_ref[sl, :]
        ms = jnp.mean(xf * xf, axis=-1, keepdims=True)
        o_ref[sl, :] = (xf * lax.rsqrt(ms + EPS) * g).astype(o_ref.dtype)
        return carry

    lax.fori_loop(0, rows // NORM_ROWS, body, 0)


def _in_proj_kernel(x_ref, g_ref, w_ref, o_ref, xn_ref):
    @pl.when(pl.program_id(1) == 0)
    def _():
        _rms_rows(x_ref, g_ref, xn_ref, TM)

    o_ref[...] = jnp.dot(xn_ref[...], w_ref[...],
                         preferred_element_type=jnp.float32).astype(o_ref.dtype)


def in_proj(x, g, w):
    return pl.pallas_call(
        _in_proj_kernel,
        grid=(N_TOK // TM, D_IN_PAD // TN_IN),
        in_specs=[pl.BlockSpec((TM, D_MODEL), lambda i, j: (i, 0)),
                  pl.BlockSpec((1, D_MODEL), lambda i, j: (0, 0)),
                  pl.BlockSpec((D_MODEL, TN_IN), lambda i, j: (0, j))],
        out_specs=pl.BlockSpec((TM, TN_IN), lambda i, j: (i, j)),
        out_shape=jax.ShapeDtypeStruct((N_TOK, D_IN_PAD), jnp.bfloat16),
        scratch_shapes=[pltpu.VMEM((TM, D_MODEL), jnp.bfloat16)],
        compiler_params=_params(("arbitrary", "arbitrary")),
        name="in_proj",
    )(x, g, w)


N_PROMPT_TILES = N_PROMPT // TM_OUT


def _out_proj_kernel(mixp_ref, mixs_ref, w_ref, x_ref, g_ref, h_ref, hn_ref):
    def run(mix_ref):
        h_ref[...] = x_ref[...] + jnp.dot(mix_ref[...], w_ref[...],
                                          preferred_element_type=jnp.float32)
        _rms_rows(h_ref, g_ref, hn_ref, TM_OUT)

    is_prompt = pl.program_id(0) < N_PROMPT_TILES
    pl.when(is_prompt)(lambda: run(mixp_ref))
    pl.when(jnp.logical_not(is_prompt))(lambda: run(mixs_ref))


def out_proj(mix_p, mix_s, w, x, g):
    return pl.pallas_call(
        _out_proj_kernel,
        grid=(N_TOK // TM_OUT,),
        in_specs=[pl.BlockSpec((TM_OUT, D_MODEL), lambda i: (jnp.minimum(i, N_PROMPT_TILES - 1), 0)),
                  pl.BlockSpec((N_SAMPLE, D_MODEL), lambda i: (0, 0)),
                  pl.BlockSpec((D_MODEL, D_MODEL), lambda i: (0, 0)),
                  pl.BlockSpec((TM_OUT, D_MODEL), lambda i: (i, 0)),
                  pl.BlockSpec((1, D_MODEL), lambda i: (0, 0))],
        out_specs=[pl.BlockSpec((TM_OUT, D_MODEL), lambda i: (i, 0)),
                   pl.BlockSpec((TM_OUT, D_MODEL), lambda i: (i, 0))],
        out_shape=[jax.ShapeDtypeStruct((N_TOK, D_MODEL), jnp.float32),
                   jax.ShapeDtypeStruct((N_TOK, D_MODEL), jnp.bfloat16)],
        compiler_params=_params(("arbitrary",)),
        name="out_proj",
    )(mix_p, mix_s, w, x, g)


def _ffn_up_kernel(hn_ref, wg_ref, wu_ref, o_ref):
    hn = hn_ref[...]
    gate = jnp.dot(hn, wg_ref[...], preferred_element_type=jnp.float32)
    up = jnp.dot(hn, wu_ref[...], preferred_element_type=jnp.float32)
    o_ref[...] = (jax.nn.silu(gate) * up).astype(o_ref.dtype)


def ffn_up(hn, wg, wu):
    return pl.pallas_call(
        _ffn_up_kernel,
        grid=(N_TOK // TM, D_FF // TN_FF),
        in_specs=[pl.BlockSpec((TM, D_MODEL), lambda i, j: (i, 0)),
                  pl.BlockSpec((D_MODEL, TN_FF), lambda i, j: (0, j)),
                  pl.BlockSpec((D_MODEL, TN_FF), lambda i, j: (0, j))],
        out_specs=pl.BlockSpec((TM, TN_FF), lambda i, j: (i, j)),
        out_shape=jax.ShapeDtypeStruct((N_TOK, D_FF), jnp.bfloat16),
        compiler_params=_params(("arbitrary", "arbitrary")),
        name="ffn_up",
    )(hn, wg, wu)


def _ffn_down_kernel(a_ref, w_ref, h_ref, o_ref):
    o_ref[...] = h_ref[...] + jnp.dot(a_ref[...], w_ref[...],
                                      preferred_element_type=jnp.float32)


def ffn_down(act, w, h):
    return pl.pallas_call(
        _ffn_down_kernel,
        grid=(N_TOK // TM, D_MODEL // TN_DOWN),
        in_specs=[pl.BlockSpec((TM, D_FF), lambda i, j: (i, 0)),
                  pl.BlockSpec((D_FF, TN_DOWN), lambda i, j: (0, j)),
                  pl.BlockSpec((TM, TN_DOWN), lambda i, j: (i, j))],
        out_specs=pl.BlockSpec((TM, TN_DOWN), lambda i, j: (i, j)),
        out_shape=jax.ShapeDtypeStruct((N_TOK, D_MODEL), jnp.float32),
        compiler_params=_params(("arbitrary", "arbitrary")),
        name="ffn_down",
    )(act, w, h)


def _final_norm_kernel(x_ref, g_ref, o_ref):
    _rms_rows(x_ref, g_ref, o_ref, TM_OUT)


def final_norm(x, g):
    return pl.pallas_call(
        _final_norm_kernel,
        grid=(N_TOK // TM_OUT,),
        in_specs=[pl.BlockSpec((TM_OUT, D_MODEL), lambda i: (i, 0)),
                  pl.BlockSpec((1, D_MODEL), lambda i: (0, 0))],
        out_specs=pl.BlockSpec((TM_OUT, D_MODEL), lambda i: (i, 0)),
        out_shape=jax.ShapeDtypeStruct((N_TOK, D_MODEL), jnp.float32),
        compiler_params=_params(("arbitrary",)),
        name="final_norm",
    )(x, g)


def _log_sigmoid(z):
    return jnp.minimum(z, 0.0) - jnp.log1p(jnp.exp(-jnp.abs(z)))


def _gelu(x):
    return 0.5 * x * (1.0 + lax.erf(x * SQRT_HALF))


def _layernorm(x, g, b):
    mu = jnp.mean(x, axis=-1, keepdims=True)
    xc = x - mu
    return xc * lax.rsqrt(jnp.mean(xc * xc, axis=-1, keepdims=True) + EPS) * g + b


def _dot(a, b):
    return jnp.dot(a, b, preferred_element_type=jnp.float32)


def _dot_nt(a, b):
    return lax.dot_general(a, b, (((1,), (1,)), ((), ())), preferred_element_type=jnp.float32)


def _dot_tn(a, b):
    return lax.dot_general(a, b, (((0,), (0,)), ((), ())), preferred_element_type=jnp.float32)


def _tril(n):
    r = lax.broadcasted_iota(jnp.int32, (n, n), 0)
    c = lax.broadcasted_iota(jnp.int32, (n, n), 1)
    return r >= c


def _prompt_mixer_kernel(proj_ref, wa2_ref, ba_ref, glag_ref, convw_ref, convb_ref, lng_ref, lnb_ref,
                         cmw_ref, cmb_ref, mix_ref, sout_ref, cout_ref, s_ref, zprev_ref):
    f32, bf16 = jnp.float32, jnp.bfloat16
    t_tile = pl.program_id(1)

    @pl.when(t_tile == 0)
    def _():
        s_ref[...] = jnp.zeros_like(s_ref)
        zprev_ref[...] = jnp.zeros_like(zprev_ref)

    causal = _tril(GLA_CHUNK)
    ones_tril = causal.astype(f32)
    q_scale = GLA_DK ** -0.5

    def gla_chunk(c, carry):
        rows = pl.ds(pl.multiple_of(c * GLA_CHUNK, GLA_CHUNK), GLA_CHUNK)
        z = _dot(proj_ref[rows, OFF_A:OFF_A + A_PAD], wa2_ref[...]) + ba_ref[...]
        la = _log_sigmoid(z) * (1.0 / GLA_TAU)
        b = jnp.dot(ones_tril, la, precision=lax.Precision.HIGHEST,
                    preferred_element_type=f32)
        g = b[GLA_CHUNK - 1:GLA_CHUNK, :]
        q = proj_ref[rows, OFF_Q:OFF_K].astype(f32) * q_scale
        k = proj_ref[rows, OFF_K:OFF_V].astype(f32)
        qt = (q * jnp.exp(b)).astype(bf16)
        kt = (k * jnp.exp(-b)).astype(bf16)
        kd = (k * jnp.exp(g - b)).astype(bf16)
        eg = jnp.exp(g)
        for h in range(GLA_HEADS):
            dk = slice(h * GLA_DK, (h + 1) * GLA_DK)
            dv = slice(h * GLA_DV, (h + 1) * GLA_DV)
            v_h = proj_ref[rows, OFF_V + h * GLA_DV:OFF_V + (h + 1) * GLA_DV]
            s_h = s_ref[h]
            a = jnp.where(causal, _dot_nt(qt[:, dk], kt[:, dk]), 0.0).astype(bf16)
            o = _dot(a, v_h) + _dot(qt[:, dk], s_h.astype(bf16))
            ds = _dot_tn(kd[:, dk], v_h)
            eg_rows = jnp.transpose(jnp.broadcast_to(eg[:, dk], (GLA_DK, GLA_DK)))
            s_ref[h] = jnp.concatenate([eg_rows, eg_rows], axis=1) * s_h + ds
            o = o * lax.rsqrt(jnp.mean(o * o, axis=-1, keepdims=True) + EPS) * glag_ref[:, dv]
            r_h = proj_ref[rows, OFF_R + h * GLA_DV:OFF_R + (h + 1) * GLA_DV].astype(f32)
            mix_ref[rows, dv] = (o * jax.nn.silu(r_h)).astype(bf16)
        return carry

    lax.fori_loop(0, TT // GLA_CHUNK, gla_chunk, 0)

    w_tril = _tril(CM_CHUNK)
    wm = [jnp.where(w_tril, cmw_ref[h], 0.0).astype(bf16) for h in range(CM_HEADS)]
    row_id = lax.broadcasted_iota(jnp.int32, (CM_CHUNK, D_CONV), 0)

    def cm_block(i, carry):
        rows = pl.ds(pl.multiple_of(i * CM_CHUNK, CM_CHUNK), CM_CHUNK)
        z = proj_ref[rows, OFF_CC:OFF_CH].astype(f32) * proj_ref[rows, OFF_CH:OFF_CU].astype(f32)
        prev = zprev_ref[...]
        p1 = prev[SUBLANES_V7X - 1:SUBLANES_V7X, :]
        p2 = prev[SUBLANES_V7X - 2:SUBLANES_V7X - 1, :]
        z1 = jnp.where(row_id == 0, p1, pltpu.roll(z, 1, 0))
        z2 = jnp.where(row_id == 0, p2, jnp.where(row_id == 1, p1, pltpu.roll(z, 2, 0)))
        conv = convb_ref[...] + convw_ref[0:1, :] * z2
        conv = conv + convw_ref[1:2, :] * z1
        conv = conv + convw_ref[2:3, :] * z
        cb = proj_ref[rows, OFF_CB:OFF_CC].astype(f32)
        mix_ref[rows, MIX_CONV:MIX_CM] = (cb * conv).astype(bf16)
        zprev_ref[...] = z[CM_CHUNK - SUBLANES_V7X:, :]
        cout_ref[0] = z[CM_CHUNK - (CONV_W - 1):, :]

        u = _gelu(proj_ref[rows, OFF_CU:OFF_CV].astype(f32))
        vv = _layernorm(_gelu(proj_ref[rows, OFF_CV:OFF_A].astype(f32)), lng_ref[...], lnb_ref[...])
        vvb = vv.astype(bf16)
        for h in range(CM_HEADS):
            hd = slice(h * CM_HD, (h + 1) * CM_HD)
            zc = _dot(wm[h], vvb[:, hd]) + cmb_ref[:, hd]
            mix_ref[rows, MIX_CM + h * CM_HD:MIX_CM + (h + 1) * CM_HD] = (u[:, hd] * zc).astype(bf16)
        return carry

    lax.fori_loop(0, TT // CM_CHUNK, cm_block, 0)

    @pl.when(t_tile == pl.num_programs(1) - 1)
    def _():
        sout_ref[0] = s_ref[...]


def prompt_mixer(proj, wa2, ba, glag, convw, convb, lng, lnb, cmw, cmb):
    const = lambda shape: pl.BlockSpec(shape, lambda b, t: (0,) * len(shape))
    tiles = SEQ // TT
    return pl.pallas_call(
        _prompt_mixer_kernel,
        grid=(BATCH, tiles),
        in_specs=[pl.BlockSpec((TT, D_IN_PAD), lambda b, t: (b * tiles + t, 0)),
                  const((A_PAD, GLA_HEADS * GLA_DK)), const((1, GLA_HEADS * GLA_DK)),
                  const((1, D_GLA)), const((CONV_W, D_CONV)), const((1, D_CONV)),
                  const((1, D_CM)), const((1, D_CM)),
                  const((CM_HEADS, CM_CHUNK, CM_CHUNK)), const((CM_CHUNK, D_CM))],
        out_specs=[pl.BlockSpec((TT, D_MODEL), lambda b, t: (b * tiles + t, 0)),
                   pl.BlockSpec((1, GLA_HEADS, GLA_DK, GLA_DV), lambda b, t: (b, 0, 0, 0)),
                   pl.BlockSpec((1, CONV_W - 1, D_CONV), lambda b, t: (b, 0, 0))],
        out_shape=[jax.ShapeDtypeStruct((N_PROMPT, D_MODEL), jnp.bfloat16),
                   jax.ShapeDtypeStruct((BATCH, GLA_HEADS, GLA_DK, GLA_DV), jnp.float32),
                   jax.ShapeDtypeStruct((BATCH, CONV_W - 1, D_CONV), jnp.float32)],
        scratch_shapes=[pltpu.VMEM((GLA_HEADS, GLA_DK, GLA_DV), jnp.float32),
                        pltpu.VMEM((SUBLANES_V7X, D_CONV), jnp.float32)],
        compiler_params=_params(("arbitrary", "arbitrary")),
        name="prompt_mixer",
    )(proj, wa2, ba, glag, convw, convb, lng, lnb, cmw, cmb)


def _gla_chunked(q, k, v, loga, S0):
    B, L = q.shape[0], q.shape[1]
    C = math.gcd(L, GLA_CHUNK)
    N = L // C

    def blk(t):
        return t.reshape(B, N, C, GLA_HEADS, t.shape[-1]).transpose(0, 3, 1, 2, 4)

    qb, kb, vb, ab = blk(q), blk(k), blk(v), blk(loga)
    b = jnp.cumsum(ab, axis=3)
    g = b[:, :, :, -1]
    qt = qb * jnp.exp(b)
    kt = kb * jnp.exp(-b)
    kd = kb * jnp.exp(g[:, :, :, None] - b)
    mask = jnp.tril(jnp.ones((C, C), dtype=bool))
    A = jnp.where(mask, jnp.einsum('bhncd,bhnsd->bhncs', qt, kt), 0.0)
    o_intra = jnp.einsum('bhncs,bhnse->bhnce', A, vb)
    dS = jnp.einsum('bhncd,bhnce->bhnde', kd, vb)

    def step(S, inp):
        g_n, dS_n = inp
        return jnp.exp(g_n)[..., None] * S + dS_n, S

    S_fin, S_prev = lax.scan(step, S0, (jnp.moveaxis(g, 2, 0), jnp.moveaxis(dS, 2, 0)))
    S_prev = jnp.moveaxis(S_prev, 0, 2)
    o = o_intra + jnp.einsum('bhncd,bhnde->bhnce', qt, S_prev)
    o = o.transpose(0, 2, 3, 1, 4).reshape(B, L, GLA_HEADS, GLA_DV)
    return o, S_fin


def _mixer(proj, S0, conv_buf, w_a2, b_a, gla_g, conv_w, conv_b, cm_ln_g, cm_ln_b, cm_ws, cm_bs):
    B, L, _ = proj.shape
    p = proj.astype(jnp.float32)
    q, k = p[..., OFF_Q:OFF_K], p[..., OFF_K:OFF_V]
    v, r = p[..., OFF_V:OFF_R], p[..., OFF_R:OFF_CB]
    cb, cc, ch = p[..., OFF_CB:OFF_CC], p[..., OFF_CC:OFF_CH], p[..., OFF_CH:OFF_CU]
    cu, cv = p[..., OFF_CU:OFF_CV], p[..., OFF_CV:OFF_A]
    a_lr = p[..., OFF_A:OFF_A + GLA_RANK]

    loga = jax.nn.log_sigmoid(a_lr @ w_a2 + b_a) / GLA_TAU
    o, S_new = _gla_chunked((q * (GLA_DK ** -0.5)).reshape(B, L, GLA_HEADS, GLA_DK),
                            k.reshape(B, L, GLA_HEADS, GLA_DK),
                            v.reshape(B, L, GLA_HEADS, GLA_DV),
                            loga.reshape(B, L, GLA_HEADS, GLA_DK), S0)
    o = o * lax.rsqrt(jnp.mean(o * o, axis=-1, keepdims=True) + EPS)
    o = o * gla_g.reshape(GLA_HEADS, GLA_DV)
    o_gla = o.reshape(B, L, D_GLA) * jax.nn.silu(r)

    z = cc * ch
    zp = jnp.concatenate([conv_buf, z], axis=1)
    conv = conv_b
    for j in range(CONV_W):
        conv = conv + conv_w[j] * zp[:, j:j + L]
    o_conv = cb * conv
    conv_new = zp[:, L:]

    u = jax.nn.gelu(cu, approximate=False)
    vv = _layernorm(jax.nn.gelu(cv, approximate=False), cm_ln_g, cm_ln_b)
    T = min(L, CM_CHUNK)
    N = L // T
    vb = vv.reshape(B, N, T, CM_HEADS, CM_HD)
    Wm = jnp.where(jnp.tril(jnp.ones((T, T), dtype=bool)), cm_ws[:, :T, :T], 0.0)
    zc = jnp.einsum('hts,bnshc->bnthc', Wm, vb) + jnp.transpose(cm_bs[:, :T])[None, None, :, :, None]
    o_cm = u * zc.reshape(B, L, D_CM)

    mix = jnp.concatenate([o_gla, o_conv, o_cm], axis=-1).astype(jnp.bfloat16)
    return mix, S_new, conv_new, vv


def _time_major(x):
    return jnp.swapaxes(x, 0, 1).reshape(N_SAMPLE, x.shape[-1])


def _batch_major(x):
    return jnp.swapaxes(x.reshape(DEC_SEQ, DEC_BATCH, x.shape[-1]), 0, 1)


def _pack_w_in(w_in):
    a0 = OFF_CB
    a1 = a0 + GLA_RANK
    pad = jnp.zeros(w_in.shape[:2] + (A_PAD - GLA_RANK,), w_in.dtype)
    return jnp.concatenate([w_in[..., :a0], w_in[..., a1:], w_in[..., a0:a1], pad],
                           axis=-1).astype(jnp.bfloat16)


def kernel(x_prompt, x_sample, state_gla, state_conv, norm1_g, w_in, w_a2, b_a, gla_g, conv_w, conv_b,
           cm_ln_g, cm_ln_b, cm_ws, cm_bs, w_out, norm2_g, w_gate, w_up, w_down, final_g):
    bf = jnp.bfloat16
    w_in_p = _pack_w_in(w_in)
    w_out_b, w_gate_b, w_up_b, w_down_b = (w.astype(bf) for w in (w_out, w_gate, w_up, w_down))
    wa2_p = jnp.pad(w_a2, ((0, 0), (0, A_PAD - GLA_RANK), (0, 0))).astype(bf)
    cmb_rows = jnp.repeat(jnp.swapaxes(cm_bs, 1, 2), CM_HD, axis=2)
    row = lambda a: a.reshape(1, -1)

    x = jnp.concatenate([x_prompt.reshape(N_PROMPT, D_MODEL), _time_major(x_sample)], axis=0)
    gla_p, conv_p, gla_s, conv_s, cmv_s = [], [], [], [], []
    for l in range(DEPTH):
        lw = (w_a2[l], b_a[l], gla_g[l], conv_w[l], conv_b[l], cm_ln_g[l], cm_ln_b[l], cm_ws[l], cm_bs[l])
        proj = in_proj(x, row(norm1_g[l]), w_in_p[l])
        mix_p, Sp, cp = prompt_mixer(proj, wa2_p[l], row(b_a[l]), row(gla_g[l]), conv_w[l], row(conv_b[l]),
                                     row(cm_ln_g[l]), row(cm_ln_b[l]), cm_ws[l], cmb_rows[l])
        mix_s, Ss, cs, vs = _mixer(_batch_major(proj[N_PROMPT:]), state_gla[l], state_conv[l], *lw)
        h, hn = out_proj(mix_p, _time_major(mix_s), w_out_b[l], x, row(norm2_g[l]))
        act = ffn_up(hn, w_gate_b[l], w_up_b[l])
        x = ffn_down(act, w_down_b[l], h)
        gla_p.append(Sp); conv_p.append(cp)
        gla_s.append(Ss); conv_s.append(cs); cmv_s.append(vs)
    y = final_norm(x, row(final_g))
    y_prompt = y[:N_PROMPT].reshape(BATCH, SEQ, D_MODEL)
    y_sample = _batch_major(y[N_PROMPT:])
    return (y_prompt, y_sample, jnp.stack(gla_p), jnp.stack(conv_p), jnp.stack(gla_s),
            jnp.stack(conv_s), jnp.stack(cmv_s))
```

```python
import math

import jax
import jax.numpy as jnp
from jax import lax
from jax.experimental import pallas as pl
from jax.experimental.pallas import tpu as pltpu

D_MODEL = 2048
BATCH = 4
SEQ = 2048
DEPTH = 4
DEC_BATCH = 128
DEC_SEQ = 4
D_GLA = 1024
GLA_HEADS = 4
GLA_DK = 128
GLA_DV = 256
GLA_RANK = 16
GLA_TAU = 16.0
GLA_CHUNK = 64
D_CONV = 512
CONV_W = 3
D_CM = 512
CM_HEADS = 4
CM_HD = 128
CM_CHUNK = 128
D_FF = 5632
EPS = 1e-6
SQRT_HALF = math.sqrt(0.5)

N_PROMPT = BATCH * SEQ
N_SAMPLE = DEC_BATCH * DEC_SEQ
N_TOK = N_PROMPT + N_SAMPLE

OFF_Q, OFF_K, OFF_V, OFF_R = 0, 512, 1024, 2048
OFF_CB, OFF_CC, OFF_CH, OFF_CU, OFF_CV = 3072, 3584, 4096, 4608, 5120
OFF_A = 5632
A_PAD = 128
D_IN_PAD = OFF_A + A_PAD
MIX_CONV, MIX_CM = D_GLA, D_GLA + D_CONV

LANES_V7X = 128
SUBLANES_V7X = 8
VMEM_LIMIT_V7X = 56 * 1024 * 1024

TM = 1088
TM_OUT = 512
TN_IN = 1152
TN_FF = 512
TN_DOWN = 512
NORM_ROWS = 32
TT = 512


def _params(sem):
    return pltpu.CompilerParams(dimension_semantics=sem, vmem_limit_bytes=VMEM_LIMIT_V7X)


def _rms_rows(x_ref, g_ref, o_ref, rows):
    g = g_ref[...]

    def body(c, carry):
        sl = pl.ds(pl.multiple_of(c * NORM_ROWS, NORM_ROWS), NORM_ROWS)
        xf = x_ref[sl, :]
        ms = jnp.mean(xf * xf, axis=-1, keepdims=True)
        o_ref[sl, :] = (xf * lax.rsqrt(ms + EPS) * g).astype(o_ref.dtype)
        return carry

    lax.fori_loop(0, rows // NORM_ROWS, body, 0)


def _in_proj_kernel(x_ref, g_ref, w_ref, o_ref, xn_ref):
    @pl.when(pl.program_id(1) == 0)
    def _():
        _rms_rows(x_ref, g_ref, xn_ref, TM)

    o_ref[...] = jnp.dot(xn_ref[...], w_ref[...],
                         preferred_element_type=jnp.float32).astype(o_ref.dtype)


def in_proj(x, g, w, l):
    return pl.pallas_call(
        _in_proj_kernel,
        grid=(N_TOK // TM, D_IN_PAD // TN_IN),
        in_specs=[pl.BlockSpec((TM, D_MODEL), lambda i, j: (i, 0)),
                  pl.BlockSpec((1, D_MODEL), lambda i, j: (0, 0)),
                  pl.BlockSpec((None, D_MODEL, TN_IN), lambda i, j: (l, 0, j))],
        out_specs=pl.BlockSpec((TM, TN_IN), lambda i, j: (i, j)),
        out_shape=jax.ShapeDtypeStruct((N_TOK, D_IN_PAD), jnp.bfloat16),
        scratch_shapes=[pltpu.VMEM((TM, D_MODEL), jnp.bfloat16)],
        compiler_params=_params(("arbitrary", "arbitrary")),
        name="in_proj",
    )(x, g, w)


N_PROMPT_TILES = N_PROMPT // TM_OUT


def _out_proj_kernel(mixp_ref, mixs_ref, w_ref, x_ref, g_ref, h_ref, hn_ref):
    def run(mix_ref):
        h_ref[...] = x_ref[...] + jnp.dot(mix_ref[...], w_ref[...],
                                          preferred_element_type=jnp.float32)
        _rms_rows(h_ref, g_ref, hn_ref, TM_OUT)

    is_prompt = pl.program_id(0) < N_PROMPT_TILES
    pl.when(is_prompt)(lambda: run(mixp_ref))
    pl.when(jnp.logical_not(is_prompt))(lambda: run(mixs_ref))


def out_proj(mix_p, mix_s, w, x, g, l):
    return pl.pallas_call(
        _out_proj_kernel,
        grid=(N_TOK // TM_OUT,),
        in_specs=[pl.BlockSpec((TM_OUT, D_MODEL), lambda i: (jnp.minimum(i, N_PROMPT_TILES - 1), 0)),
                  pl.BlockSpec((N_SAMPLE, D_MODEL), lambda i: (0, 0)),
                  pl.BlockSpec((None, D_MODEL, D_MODEL), lambda i: (l, 0, 0)),
                  pl.BlockSpec((TM_OUT, D_MODEL), lambda i: (i, 0)),
                  pl.BlockSpec((1, D_MODEL), lambda i: (0, 0))],
        out_specs=[pl.BlockSpec((TM_OUT, D_MODEL), lambda i: (i, 0)),
                   pl.BlockSpec((TM_OUT, D_MODEL), lambda i: (i, 0))],
        out_shape=[jax.ShapeDtypeStruct((N_TOK, D_MODEL), jnp.float32),
                   jax.ShapeDtypeStruct((N_TOK, D_MODEL), jnp.bfloat16)],
        compiler_params=_params(("arbitrary",)),
        name="out_proj",
    )(mix_p, mix_s, w, x, g)


def _ffn_up_kernel(hn_ref, wg_ref, wu_ref, o_ref, wgb_ref, wub_ref):
    @pl.when(pl.program_id(1) == 0)
    def _():
        wgb_ref[...] = wg_ref[...].astype(jnp.bfloat16)
        wub_ref[...] = wu_ref[...].astype(jnp.bfloat16)

    hn = hn_ref[...]
    gate = jnp.dot(hn, wgb_ref[...], preferred_element_type=jnp.float32)
    up = jnp.dot(hn, wub_ref[...], preferred_element_type=jnp.float32)
    o_ref[...] = (jax.nn.silu(gate) * up).astype(o_ref.dtype)


def ffn_up(hn, wg, wu, l):
    return pl.pallas_call(
        _ffn_up_kernel,
        grid=(D_FF // TN_FF, N_TOK // TM),
        in_specs=[pl.BlockSpec((TM, D_MODEL), lambda j, i: (i, 0)),
                  pl.BlockSpec((None, D_MODEL, TN_FF), lambda j, i: (l, 0, j)),
                  pl.BlockSpec((None, D_MODEL, TN_FF), lambda j, i: (l, 0, j))],
        out_specs=pl.BlockSpec((TM, TN_FF), lambda j, i: (i, j)),
        out_shape=jax.ShapeDtypeStruct((N_TOK, D_FF), jnp.bfloat16),
        scratch_shapes=[pltpu.VMEM((D_MODEL, TN_FF), jnp.bfloat16),
                        pltpu.VMEM((D_MODEL, TN_FF), jnp.bfloat16)],
        compiler_params=_params(("arbitrary", "arbitrary")),
        name="ffn_up",
    )(hn, wg, wu)


def _ffn_down_kernel(a_ref, w_ref, h_ref, o_ref):
    o_ref[...] = h_ref[...] + jnp.dot(a_ref[...], w_ref[...],
                                      preferred_element_type=jnp.float32)


def ffn_down(act, w, h, l):
    return pl.pallas_call(
        _ffn_down_kernel,
        grid=(N_TOK // TM, D_MODEL // TN_DOWN),
        in_specs=[pl.BlockSpec((TM, D_FF), lambda i, j: (i, 0)),
                  pl.BlockSpec((None, D_FF, TN_DOWN), lambda i, j: (l, 0, j)),
                  pl.BlockSpec((TM, TN_DOWN), lambda i, j: (i, j))],
        out_specs=pl.BlockSpec((TM, TN_DOWN), lambda i, j: (i, j)),
        out_shape=jax.ShapeDtypeStruct((N_TOK, D_MODEL), jnp.float32),
        compiler_params=_params(("arbitrary", "arbitrary")),
        name="ffn_down",
    )(act, w, h)


def _final_norm_kernel(x_ref, g_ref, yp_ref, ys_ref):
    is_prompt = pl.program_id(0) < N_PROMPT_TILES
    pl.when(is_prompt)(lambda: _rms_rows(x_ref, g_ref, yp_ref, TM_OUT))
    pl.when(jnp.logical_not(is_prompt))(lambda: _rms_rows(x_ref, g_ref, ys_ref, TM_OUT))


def final_norm(x, g):
    return pl.pallas_call(
        _final_norm_kernel,
        grid=(N_TOK // TM_OUT,),
        in_specs=[pl.BlockSpec((TM_OUT, D_MODEL), lambda i: (i, 0)),
                  pl.BlockSpec((1, D_MODEL), lambda i: (0, 0))],
        out_specs=[pl.BlockSpec((TM_OUT, D_MODEL), lambda i: (jnp.minimum(i, N_PROMPT_TILES - 1), 0)),
                   pl.BlockSpec((N_SAMPLE, D_MODEL), lambda i: (0, 0))],
        out_shape=[jax.ShapeDtypeStruct((N_PROMPT, D_MODEL), jnp.float32),
                   jax.ShapeDtypeStruct((N_SAMPLE, D_MODEL), jnp.float32)],
        compiler_params=_params(("arbitrary",)),
        name="final_norm",
    )(x, g)


def _log_sigmoid(z):
    return jnp.minimum(z, 0.0) - jnp.log1p(jnp.exp(-jnp.abs(z)))


def _gelu(x):
    return 0.5 * x * (1.0 + lax.erf(x * SQRT_HALF))


def _layernorm(x, g, b):
    mu = jnp.mean(x, axis=-1, keepdims=True)
    xc = x - mu
    return xc * lax.rsqrt(jnp.mean(xc * xc, axis=-1, keepdims=True) + EPS) * g + b


def _dot(a, b):
    return jnp.dot(a, b, preferred_element_type=jnp.float32)


def _dot_nt(a, b):
    return lax.dot_general(a, b, (((1,), (1,)), ((), ())), preferred_element_type=jnp.float32)


def _dot_tn(a, b):
    return lax.dot_general(a, b, (((0,), (0,)), ((), ())), preferred_element_type=jnp.float32)


def _tril(n):
    r = lax.broadcasted_iota(jnp.int32, (n, n), 0)
    c = lax.broadcasted_iota(jnp.int32, (n, n), 1)
    return r >= c


def _prompt_mixer_kernel(proj_ref, wa2_ref, ba_ref, glag_ref, convw_ref, convb_ref, lng_ref, lnb_ref,
                         cmw_ref, cmb_ref, mix_ref, sout_ref, cout_ref, s_ref, zprev_ref):
    f32, bf16 = jnp.float32, jnp.bfloat16
    t_tile = pl.program_id(1)

    @pl.when(t_tile == 0)
    def _():
        s_ref[...] = jnp.zeros_like(s_ref)
        zprev_ref[...] = jnp.zeros_like(zprev_ref)

    causal = _tril(GLA_CHUNK)
    ones_tril = causal.astype(f32)
    q_scale = GLA_DK ** -0.5

    def gla_chunk(c, carry):
        rows = pl.ds(pl.multiple_of(c * GLA_CHUNK, GLA_CHUNK), GLA_CHUNK)
        z = _dot(proj_ref[rows, OFF_A:OFF_A + A_PAD], wa2_ref[...]) + ba_ref[...]
        la = _log_sigmoid(z) * (1.0 / GLA_TAU)
        b = jnp.dot(ones_tril, la, precision=lax.Precision.HIGHEST,
                    preferred_element_type=f32)
        g = b[GLA_CHUNK - 1:GLA_CHUNK, :]
        q = proj_ref[rows, OFF_Q:OFF_K].astype(f32) * q_scale
        k = proj_ref[rows, OFF_K:OFF_V].astype(f32)
        qt = (q * jnp.exp(b)).astype(bf16)
        kt = (k * jnp.exp(-b)).astype(bf16)
        kd = (k * jnp.exp(g - b)).astype(bf16)
        eg = jnp.exp(g)
        for h in range(GLA_HEADS):
            dk = slice(h * GLA_DK, (h + 1) * GLA_DK)
            dv = slice(h * GLA_DV, (h + 1) * GLA_DV)
            v_h = proj_ref[rows, OFF_V + h * GLA_DV:OFF_V + (h + 1) * GLA_DV]
            s_h = s_ref[h]
            a = jnp.where(causal, _dot_nt(qt[:, dk], kt[:, dk]), 0.0).astype(bf16)
            o = _dot(a, v_h) + _dot(qt[:, dk], s_h.astype(bf16))
            ds = _dot_tn(kd[:, dk], v_h)
            eg_rows = jnp.transpose(jnp.broadcast_to(eg[:, dk], (GLA_DK, GLA_DK)))
            s_ref[h] = jnp.concatenate([eg_rows, eg_rows], axis=1) * s_h + ds
            o = o * lax.rsqrt(jnp.mean(o * o, axis=-1, keepdims=True) + EPS) * glag_ref[:, dv]
            r_h = proj_ref[rows, OFF_R + h * GLA_DV:OFF_R + (h + 1) * GLA_DV].astype(f32)
            mix_ref[rows, dv] = (o * jax.nn.silu(r_h)).astype(bf16)
        return carry

    lax.fori_loop(0, TT // GLA_CHUNK, gla_chunk, 0, unroll=4)

    w_tril = _tril(CM_CHUNK)
    wm = [jnp.where(w_tril, cmw_ref[h], 0.0).astype(bf16) for h in range(CM_HEADS)]
    row_id = lax.broadcasted_iota(jnp.int32, (CM_CHUNK, D_CONV), 0)

    def cm_block(i, carry):
        rows = pl.ds(pl.multiple_of(i * CM_CHUNK, CM_CHUNK), CM_CHUNK)
        z = proj_ref[rows, OFF_CC:OFF_CH].astype(f32) * proj_ref[rows, OFF_CH:OFF_CU].astype(f32)
        prev = zprev_ref[...]
        p1 = prev[SUBLANES_V7X - 1:SUBLANES_V7X, :]
        p2 = prev[SUBLANES_V7X - 2:SUBLANES_V7X - 1, :]
        z1 = jnp.where(row_id == 0, p1, pltpu.roll(z, 1, 0))
        z2 = jnp.where(row_id == 0, p2, jnp.where(row_id == 1, p1, pltpu.roll(z, 2, 0)))
        conv = convb_ref[...] + convw_ref[0:1, :] * z2
        conv = conv + convw_ref[1:2, :] * z1
        conv = conv + convw_ref[2:3, :] * z
        cb = proj_ref[rows, OFF_CB:OFF_CC].astype(f32)
        mix_ref[rows, MIX_CONV:MIX_CM] = (cb * conv).astype(bf16)
        zprev_ref[...] = z[CM_CHUNK - SUBLANES_V7X:, :]
        cout_ref[0] = z[CM_CHUNK - (CONV_W - 1):, :]

        u = _gelu(proj_ref[rows, OFF_CU:OFF_CV].astype(f32))
        vv = _layernorm(_gelu(proj_ref[rows, OFF_CV:OFF_A].astype(f32)), lng_ref[...], lnb_ref[...])
        vvb = vv.astype(bf16)
        for h in range(CM_HEADS):
            hd = slice(h * CM_HD, (h + 1) * CM_HD)
            zc = _dot(wm[h], vvb[:, hd]) + cmb_ref[:, hd]
            mix_ref[rows, MIX_CM + h * CM_HD:MIX_CM + (h + 1) * CM_HD] = (u[:, hd] * zc).astype(bf16)
        return carry

    lax.fori_loop(0, TT // CM_CHUNK, cm_block, 0, unroll=2)

    @pl.when(t_tile == pl.num_programs(1) - 1)
    def _():
        sout_ref[0] = s_ref[...]


def prompt_mixer(proj, wa2, ba, glag, convw, convb, lng, lnb, cmw, cmb):
    const = lambda shape: pl.BlockSpec(shape, lambda b, t: (0,) * len(shape))
    tiles = SEQ // TT
    return pl.pallas_call(
        _prompt_mixer_kernel,
        grid=(BATCH, tiles),
        in_specs=[pl.BlockSpec((TT, D_IN_PAD), lambda b, t: (b * tiles + t, 0)),
                  const((A_PAD, GLA_HEADS * GLA_DK)), const((1, GLA_HEADS * GLA_DK)),
                  const((1, D_GLA)), const((CONV_W, D_CONV)), const((1, D_CONV)),
                  const((1, D_CM)), const((1, D_CM)),
                  const((CM_HEADS, CM_CHUNK, CM_CHUNK)), const((CM_CHUNK, D_CM))],
        out_specs=[pl.BlockSpec((TT, D_MODEL), lambda b, t: (b * tiles + t, 0)),
                   pl.BlockSpec((1, GLA_HEADS, GLA_DK, GLA_DV), lambda b, t: (b, 0, 0, 0)),
                   pl.BlockSpec((1, CONV_W - 1, D_CONV), lambda b, t: (b, 0, 0))],
        out_shape=[jax.ShapeDtypeStruct((N_PROMPT, D_MODEL), jnp.bfloat16),
                   jax.ShapeDtypeStruct((BATCH, GLA_HEADS, GLA_DK, GLA_DV), jnp.float32),
                   jax.ShapeDtypeStruct((BATCH, CONV_W - 1, D_CONV), jnp.float32)],
        scratch_shapes=[pltpu.VMEM((GLA_HEADS, GLA_DK, GLA_DV), jnp.float32),
                        pltpu.VMEM((SUBLANES_V7X, D_CONV), jnp.float32)],
        compiler_params=_params(("arbitrary", "arbitrary")),
        name="prompt_mixer",
    )(proj, wa2, ba, glag, convw, convb, lng, lnb, cmw, cmb)


NB = 8
SLOT = SUBLANES_V7X


def _sample_mixer_kernel(proj_ref, sin_ref, _new_state_hbm, cbuf_ref, wa2_ref, ba_ref, glag_ref, convw_ref, convb_ref,
                         lng_ref, lnb_ref, cmw_ref, cmb_ref,
                         mix_ref, sout_ref, cout_ref, vv_ref,
                         qb_s, kb_s, vb_s, ob_s, oi_s, eg_s):
    f32, bf16 = jnp.float32, jnp.bfloat16
    step = pl.program_id(0)
    q_scale = GLA_DK ** -0.5
    trows = [slice(t * DEC_BATCH, (t + 1) * DEC_BATCH) for t in range(DEC_SEQ)]

    @pl.when(step == 0)
    def _():
        qb_s[...] = jnp.zeros_like(qb_s)
        kb_s[...] = jnp.zeros_like(kb_s)
        vb_s[...] = jnp.zeros_like(vb_s)

        cum, b = [], None
        for t in range(DEC_SEQ):
            z = _dot(proj_ref[trows[t], OFF_A:OFF_A + A_PAD], wa2_ref[...]) + ba_ref[...]
            la = _log_sigmoid(z) * (1.0 / GLA_TAU)
            b = la if b is None else b + la
            cum.append(b)
        g = cum[-1]
        eg_s[...] = jnp.exp(g)
        qt, kt, vs = [], [], []
        for t in range(DEC_SEQ):
            q = proj_ref[trows[t], OFF_Q:OFF_K].astype(f32) * q_scale
            k = proj_ref[trows[t], OFF_K:OFF_V].astype(f32)
            v = proj_ref[trows[t], OFF_V:OFF_R].astype(f32)
            qt.append(q * jnp.exp(cum[t]))
            kt.append(k * jnp.exp(-cum[t]))
            vs.append(v)
            seq_rows = pl.ds(t, DEC_BATCH, stride=SLOT)
            kd = k * jnp.exp(g - cum[t])
            for j in range(GLA_HEADS):
                qb_s[j, seq_rows, :] = qt[t][:, j * LANES_V7X:(j + 1) * LANES_V7X]
                kb_s[j, seq_rows, :] = kd[:, j * LANES_V7X:(j + 1) * LANES_V7X]
            for j in range(D_GLA // LANES_V7X):
                vb_s[j, seq_rows, :] = v[:, j * LANES_V7X:(j + 1) * LANES_V7X]
        for t in range(DEC_SEQ):
            heads = []
            for h in range(GLA_HEADS):
                dk = slice(h * GLA_DK, (h + 1) * GLA_DK)
                dv = slice(h * GLA_DV, (h + 1) * GLA_DV)
                acc = None
                for s in range(t + 1):
                    a_ts = jnp.sum(qt[t][:, dk] * kt[s][:, dk], axis=-1, keepdims=True)
                    term = a_ts * vs[s][:, dv]
                    acc = term if acc is None else acc + term
                heads.append(acc)
            oi_s[trows[t], :] = jnp.concatenate(heads, axis=1)

        zp = [cbuf_ref[0], cbuf_ref[1]]
        for t in range(DEC_SEQ):
            zp.append(proj_ref[trows[t], OFF_CC:OFF_CH].astype(f32)
                      * proj_ref[trows[t], OFF_CH:OFF_CU].astype(f32))
        for t in range(DEC_SEQ):
            conv = convb_ref[...] + convw_ref[0:1, :] * zp[t]
            conv = conv + convw_ref[1:2, :] * zp[t + 1]
            conv = conv + convw_ref[2:3, :] * zp[t + 2]
            cb = proj_ref[trows[t], OFF_CB:OFF_CC].astype(f32)
            mix_ref[trows[t], MIX_CONV:MIX_CM] = (cb * conv).astype(bf16)
        cout_ref[0] = zp[DEC_SEQ]
        cout_ref[1] = zp[DEC_SEQ + 1]

        vvs = []
        for t in range(DEC_SEQ):
            vv = _layernorm(_gelu(proj_ref[trows[t], OFF_CV:OFF_A].astype(f32)), lng_ref[...], lnb_ref[...])
            vv_ref[trows[t], :] = vv
            vvs.append(vv)
        for t in range(DEC_SEQ):
            zc = cmb_ref[t:t + 1, :]
            for s in range(t + 1):
                w_ts = cmw_ref[t * DEC_SEQ + s:t * DEC_SEQ + s + 1, :]
                zc = zc + w_ts * vvs[s]
            u = _gelu(proj_ref[trows[t], OFF_CU:OFF_CV].astype(f32))
            mix_ref[trows[t], MIX_CM:] = (u * zc).astype(bf16)

    def seq_body(i, carry):
        seq = step * NB + i
        slot = pl.ds(pl.multiple_of(seq * SLOT, SLOT), SLOT)
        eg_row = eg_s[pl.ds(seq, 1), :]
        for h in range(GLA_HEADS):
            dk = slice(h * GLA_DK, (h + 1) * GLA_DK)
            dv = slice(h * GLA_DV, (h + 1) * GLA_DV)
            s0 = sin_ref[i, h]
            o_seq = _dot(qb_s[h, slot, :].astype(bf16), s0.astype(bf16))
            ob_s[2 * h, slot, :] = o_seq[:, :LANES_V7X]
            ob_s[2 * h + 1, slot, :] = o_seq[:, LANES_V7X:]
            v_seq = jnp.concatenate([vb_s[2 * h, slot, :], vb_s[2 * h + 1, slot, :]], axis=1)
            ds = _dot_tn(kb_s[h, slot, :].astype(bf16), v_seq.astype(bf16))
            eg_rows = jnp.transpose(jnp.broadcast_to(eg_row[:, dk], (GLA_DK, GLA_DK)))
            sout_ref[i, h] = jnp.concatenate([eg_rows, eg_rows], axis=1) * s0 + ds
        return carry

    lax.fori_loop(0, NB, seq_body, 0, unroll=4)

    @pl.when(step == pl.num_programs(0) - 1)
    def _():
        for t in range(DEC_SEQ):
            seq_rows = pl.ds(t, DEC_BATCH, stride=SLOT)
            o_t = oi_s[trows[t], :] + jnp.concatenate(
                [ob_s[j, seq_rows, :] for j in range(D_GLA // LANES_V7X)], axis=1)
            for h in range(GLA_HEADS):
                dv = slice(h * GLA_DV, (h + 1) * GLA_DV)
                o = o_t[:, dv]
                o = o * lax.rsqrt(jnp.mean(o * o, axis=-1, keepdims=True) + EPS) * glag_ref[:, dv]
                r_h = proj_ref[trows[t], OFF_R + h * GLA_DV:OFF_R + (h + 1) * GLA_DV].astype(f32)
                mix_ref[trows[t], dv] = (o * jax.nn.silu(r_h)).astype(bf16)


def sample_mixer(proj, state, new_state, cbuf, wa2, ba, glag, convw, convb, lng, lnb, cmw, cmb, l):
    const = lambda shape: pl.BlockSpec(shape, lambda j: (0,) * len(shape))
    seq_rows = DEC_BATCH * SLOT
    state_block = pl.BlockSpec((None, NB, GLA_HEADS, GLA_DK, GLA_DV), lambda j: (l, j, 0, 0, 0))
    return pl.pallas_call(
        _sample_mixer_kernel,
        grid=(DEC_BATCH // NB,),
        in_specs=[pl.BlockSpec((N_SAMPLE, D_IN_PAD), lambda j: (N_PROMPT // N_SAMPLE, 0)),
                  state_block,
                  pl.BlockSpec(memory_space=pl.ANY),
                  const((CONV_W - 1, DEC_BATCH, D_CONV)),
                  const((A_PAD, GLA_HEADS * GLA_DK)), const((1, GLA_HEADS * GLA_DK)),
                  const((1, D_GLA)), const((CONV_W, D_CONV)), const((1, D_CONV)),
                  const((1, D_CM)), const((1, D_CM)),
                  const((DEC_SEQ * DEC_SEQ, D_CM)), const((CM_CHUNK, D_CM))],
        out_specs=[const((N_SAMPLE, D_MODEL)),
                   state_block,
                   const((CONV_W - 1, DEC_BATCH, D_CONV)),
                   const((N_SAMPLE, D_CM))],
        out_shape=[jax.ShapeDtypeStruct((N_SAMPLE, D_MODEL), jnp.bfloat16),
                   jax.ShapeDtypeStruct((DEPTH, DEC_BATCH, GLA_HEADS, GLA_DK, GLA_DV), jnp.float32),
                   jax.ShapeDtypeStruct((CONV_W - 1, DEC_BATCH, D_CONV), jnp.float32),
                   jax.ShapeDtypeStruct((N_SAMPLE, D_CM), jnp.float32)],
        scratch_shapes=[pltpu.VMEM((GLA_HEADS, seq_rows, LANES_V7X), jnp.float32),
                        pltpu.VMEM((GLA_HEADS, seq_rows, LANES_V7X), jnp.float32),
                        pltpu.VMEM((D_GLA // LANES_V7X, seq_rows, LANES_V7X), jnp.float32),
                        pltpu.VMEM((D_GLA // LANES_V7X, seq_rows, LANES_V7X), jnp.float32),
                        pltpu.VMEM((N_SAMPLE, D_GLA), jnp.float32),
                        pltpu.VMEM((DEC_BATCH, GLA_HEADS * GLA_DK), jnp.float32)],
        input_output_aliases={2: 1},
        compiler_params=_params(("arbitrary",)),
        name="sample_mixer",
    )(proj, state, new_state, cbuf, wa2, ba, glag, convw, convb, lng, lnb, cmw, cmb)


def _time_major(x):
    return jnp.swapaxes(x, 0, 1).reshape(N_SAMPLE, x.shape[-1])


def _batch_major(x):
    return jnp.swapaxes(x.reshape(DEC_SEQ, DEC_BATCH, x.shape[-1]), 0, 1)


def _pack_w_in(w_in):
    a0 = OFF_CB
    a1 = a0 + GLA_RANK
    pad = jnp.zeros(w_in.shape[:2] + (A_PAD - GLA_RANK,), w_in.dtype)
    return jnp.concatenate([w_in[..., :a0], w_in[..., a1:], w_in[..., a0:a1], pad],
                           axis=-1).astype(jnp.bfloat16)


def kernel(x_prompt, x_sample, state_gla, state_conv, norm1_g, w_in, w_a2, b_a, gla_g, conv_w, conv_b,
           cm_ln_g, cm_ln_b, cm_ws, cm_bs, w_out, norm2_g, w_gate, w_up, w_down, final_g):
    bf = jnp.bfloat16
    w_in_p = _pack_w_in(w_in)
    w_out_b, w_down_b = w_out.astype(bf), w_down.astype(bf)
    wa2_p = jnp.pad(w_a2, ((0, 0), (0, A_PAD - GLA_RANK), (0, 0))).astype(bf)
    cmb_rows = jnp.repeat(jnp.swapaxes(cm_bs, 1, 2), CM_HD, axis=2)
    cmw_small = jnp.repeat(jnp.transpose(cm_ws[:, :, :DEC_SEQ, :DEC_SEQ], (0, 2, 3, 1))
                           .reshape(DEPTH, DEC_SEQ * DEC_SEQ, CM_HEADS), CM_HD, axis=2)
    cbuf_tm = jnp.swapaxes(state_conv, 1, 2)
    row = lambda a: a.reshape(1, -1)

    x = jnp.concatenate([x_prompt.reshape(N_PROMPT, D_MODEL), _time_major(x_sample)], axis=0)
    gla_p, conv_p, conv_s, cmv_s = [], [], [], []
    gla_s = jnp.zeros(state_gla.shape, state_gla.dtype)
    for l in range(DEPTH):
        lw = (wa2_p[l], row(b_a[l]), row(gla_g[l]), conv_w[l], row(conv_b[l]), row(cm_ln_g[l]), row(cm_ln_b[l]))
        proj = in_proj(x, row(norm1_g[l]), w_in_p, l)
        mix_p, Sp, cp = prompt_mixer(proj, *lw, cm_ws[l], cmb_rows[l])
        mix_s, gla_s, cs, vs = sample_mixer(proj, state_gla, gla_s, cbuf_tm[l], *lw,
                                            cmw_small[l], cmb_rows[l], l)
        h, hn = out_proj(mix_p, mix_s, w_out_b, x, row(norm2_g[l]), l)
        act = ffn_up(hn, w_gate, w_up, l)
        x = ffn_down(act, w_down_b, h, l)
        gla_p.append(Sp); conv_p.append(cp)
        conv_s.append(jnp.swapaxes(cs, 0, 1)); cmv_s.append(_batch_major(vs))
    y_p, y_s = final_norm(x, row(final_g))
    y_prompt = y_p.reshape(BATCH, SEQ, D_MODEL)
    y_sample = _batch_major(y_s)
    return (y_prompt, y_sample, jnp.stack(gla_p), jnp.stack(conv_p), gla_s,
            jnp.stack(conv_s), jnp.stack(cmv_s))
```

```python
import math

import jax
import jax.numpy as jnp
from jax import lax
from jax.experimental import pallas as pl
from jax.experimental.pallas import tpu as pltpu

D_MODEL = 2048
BATCH = 4
SEQ = 2048
DEPTH = 4
DEC_BATCH = 128
DEC_SEQ = 4
D_GLA = 1024
GLA_HEADS = 4
GLA_DK = 128
GLA_DV = 256
GLA_RANK = 16
GLA_TAU = 16.0
GLA_CHUNK = 64
D_CONV = 512
CONV_W = 3
D_CM = 512
CM_HEADS = 4
CM_HD = 128
CM_CHUNK = 128
D_FF = 5632
EPS = 1e-6
SQRT_HALF = math.sqrt(0.5)

N_PROMPT = BATCH * SEQ
N_SAMPLE = DEC_BATCH * DEC_SEQ
N_TOK = N_PROMPT + N_SAMPLE

OFF_Q, OFF_K, OFF_V, OFF_R = 0, 512, 1024, 2048
OFF_CB, OFF_CC, OFF_CH, OFF_CU, OFF_CV = 3072, 3584, 4096, 4608, 5120
D_MAIN = 5632
A_PAD = 128
MIX_CONV, MIX_CM = D_GLA, D_GLA + D_CONV

LANES_V7X = 128
SUBLANES_V7X = 8
VMEM_LIMIT_V7X = 56 * 1024 * 1024

TM = 1088
TM_OUT = 512
TN_IN = D_MAIN // 2
TN_FF = 512
TN_DOWN = 512
NORM_ROWS = 32
TT = 512


def _params(sem):
    return pltpu.CompilerParams(dimension_semantics=sem, vmem_limit_bytes=VMEM_LIMIT_V7X)


def _rms_rows(x_ref, g_ref, o_ref, rows):
    g = g_ref[...]

    def body(c, carry):
        sl = pl.ds(pl.multiple_of(c * NORM_ROWS, NORM_ROWS), NORM_ROWS)
        xf = x_ref[sl, :]
        ms = jnp.mean(xf * xf, axis=-1, keepdims=True)
        o_ref[sl, :] = (xf * lax.rsqrt(ms + EPS) * g).astype(o_ref.dtype)
        return carry

    lax.fori_loop(0, rows // NORM_ROWS, body, 0)


def _rms_rows_unrolled(load_rows, g, o_ref, rows):
    for c in range(rows // NORM_ROWS):
        sl = slice(c * NORM_ROWS, (c + 1) * NORM_ROWS)
        xf = load_rows(sl)
        ms = jnp.mean(xf * xf, axis=-1, keepdims=True)
        o_ref[sl, :] = (xf * lax.rsqrt(ms + EPS) * g).astype(o_ref.dtype)


def _row_tile_specs(sample_block, index):
    prompt = pl.BlockSpec((TM_OUT, D_MODEL),
                          lambda *ids: (jnp.minimum(index(*ids), N_PROMPT_TILES - 1), 0))
    sample = pl.BlockSpec((TM_OUT, D_MODEL), lambda *ids: (sample_block, 0),
                          pipeline_mode=pl.Buffered(1))
    return prompt, sample


N_PROMPT_TILES = N_PROMPT // TM_OUT
N_ROW_TILES = N_TOK // TM_OUT


def _in_proj_kernel(x0_ref, xp_ref, xs_ref, g_ref, w_ref, wa_ref, o_ref, oa_ref, xn_a, xn_b):
    j, i = pl.program_id(0), pl.program_id(1)
    g = g_ref[...]

    @pl.when((i == 0) & (j == 0))
    def _():
        _rms_rows(x0_ref, g_ref, xn_a, TM_OUT)

    next_is_sample = i == N_PROMPT_TILES - 1

    def load_next(sl):
        return jnp.where(next_is_sample, xs_ref[sl, :], xp_ref[sl, :])

    def step(cur, nxt):
        _rms_rows_unrolled(load_next, g, nxt, TM_OUT)
        o_ref[...] = jnp.dot(cur[...], w_ref[...],
                             preferred_element_type=jnp.float32).astype(o_ref.dtype)

        @pl.when(j == 0)
        def _():
            oa_ref[...] = jnp.dot(cur[...], wa_ref[...],
                                  preferred_element_type=jnp.float32).astype(oa_ref.dtype)

    even = (j * N_ROW_TILES + i) % 2 == 0
    pl.when(even)(lambda: step(xn_a, xn_b))
    pl.when(jnp.logical_not(even))(lambda: step(xn_b, xn_a))


def in_proj(xp, xs, xs_block, g, w, wa, l):
    nxt = lambda j, i: (i + 1) % N_ROW_TILES
    xp_spec, xs_spec = _row_tile_specs(xs_block, nxt)
    return pl.pallas_call(
        _in_proj_kernel,
        grid=(D_MAIN // TN_IN, N_ROW_TILES),
        in_specs=[pl.BlockSpec((TM_OUT, D_MODEL), lambda j, i: (0, 0), pipeline_mode=pl.Buffered(1)),
                  xp_spec, xs_spec,
                  pl.BlockSpec((1, D_MODEL), lambda j, i: (0, 0)),
                  pl.BlockSpec((None, D_MODEL, TN_IN), lambda j, i: (l, 0, j), pipeline_mode=pl.Buffered(1)),
                  pl.BlockSpec((None, D_MODEL, A_PAD), lambda j, i: (l, 0, 0))],
        out_specs=[pl.BlockSpec((TM_OUT, TN_IN), lambda j, i: (i, j)),
                   pl.BlockSpec((TM_OUT, A_PAD), lambda j, i: (jnp.where(j == 0, i, N_ROW_TILES - 1), 0))],
        out_shape=[jax.ShapeDtypeStruct((N_TOK, D_MAIN), jnp.bfloat16),
                   jax.ShapeDtypeStruct((N_TOK, A_PAD), jnp.bfloat16)],
        scratch_shapes=[pltpu.VMEM((TM_OUT, D_MODEL), jnp.bfloat16),
                        pltpu.VMEM((TM_OUT, D_MODEL), jnp.bfloat16)],
        compiler_params=_params(("arbitrary", "arbitrary")),
        name="in_proj",
    )(xp, xp, xs, g, w, wa)


def _out_proj_kernel(mixp_ref, mixs_ref, w_ref, xp_ref, xs_ref, g_ref, h_ref, hn_ref, hs_a, hs_b):
    i = pl.program_id(0)
    g = g_ref[...]
    is_sample = i == N_PROMPT_TILES

    def matmul(dst):
        mix = jnp.where(is_sample, mixs_ref[...], mixp_ref[...])
        x = jnp.where(is_sample, xs_ref[...], xp_ref[...])
        dst[...] = x + jnp.dot(mix, w_ref[...], preferred_element_type=jnp.float32)

    def finish(src):
        def load(sl):
            rows = src[sl, :]
            h_ref[sl, :] = rows
            return rows
        _rms_rows_unrolled(load, g, hn_ref, TM_OUT)

    odd = i % 2 == 1
    last = i == N_ROW_TILES

    @pl.when(i == 0)
    def _():
        matmul(hs_a)

    @pl.when(odd & jnp.logical_not(last))
    def _():
        matmul(hs_b)
        finish(hs_a)

    @pl.when(jnp.logical_not(odd) & (i > 0))
    def _():
        matmul(hs_a)
        finish(hs_b)

    @pl.when(last)
    def _():
        finish(hs_a)


def out_proj(mix_p, mix_s, w, xp, xs, xs_block, g, l):
    assert N_ROW_TILES % 2 == 1
    cur = lambda i: jnp.minimum(i, N_ROW_TILES - 1)
    xp_spec, xs_spec = _row_tile_specs(xs_block, cur)
    done = lambda i: (jnp.maximum(i - 1, 0), 0)
    return pl.pallas_call(
        _out_proj_kernel,
        grid=(N_ROW_TILES + 1,),
        in_specs=[pl.BlockSpec((TM_OUT, D_MODEL), lambda i: (jnp.minimum(i, N_PROMPT_TILES - 1), 0)),
                  pl.BlockSpec((N_SAMPLE, D_MODEL), lambda i: (0, 0), pipeline_mode=pl.Buffered(1)),
                  pl.BlockSpec((None, D_MODEL, D_MODEL), lambda i: (l, 0, 0), pipeline_mode=pl.Buffered(1)),
                  xp_spec, xs_spec,
                  pl.BlockSpec((1, D_MODEL), lambda i: (0, 0))],
        out_specs=[pl.BlockSpec((TM_OUT, D_MODEL), done),
                   pl.BlockSpec((TM_OUT, D_MODEL), done)],
        out_shape=[jax.ShapeDtypeStruct((N_TOK, D_MODEL), jnp.float32),
                   jax.ShapeDtypeStruct((N_TOK, D_MODEL), jnp.bfloat16)],
        scratch_shapes=[pltpu.VMEM((TM_OUT, D_MODEL), jnp.float32),
                        pltpu.VMEM((TM_OUT, D_MODEL), jnp.float32)],
        compiler_params=_params(("arbitrary",)),
        name="out_proj",
    )(mix_p, mix_s, w, xp, xs, g)


def _ffn_up_kernel(hn_ref, wg_ref, wu_ref, o_ref, wgb_ref, wub_ref):
    @pl.when(pl.program_id(1) == 0)
    def _():
        wgb_ref[...] = wg_ref[...].astype(jnp.bfloat16)
        wub_ref[...] = wu_ref[...].astype(jnp.bfloat16)

    hn = hn_ref[...]
    gate = jnp.dot(hn, wgb_ref[...], preferred_element_type=jnp.float32)
    up = jnp.dot(hn, wub_ref[...], preferred_element_type=jnp.float32)
    o_ref[...] = (jax.nn.silu(gate) * up).astype(o_ref.dtype)


def ffn_up(hn, wg, wu, l):
    return pl.pallas_call(
        _ffn_up_kernel,
        grid=(D_FF // TN_FF, N_TOK // TM),
        in_specs=[pl.BlockSpec((TM, D_MODEL), lambda j, i: (i, 0)),
                  pl.BlockSpec((None, D_MODEL, TN_FF), lambda j, i: (l, 0, j)),
                  pl.BlockSpec((None, D_MODEL, TN_FF), lambda j, i: (l, 0, j))],
        out_specs=pl.BlockSpec((TM, TN_FF), lambda j, i: (i, j)),
        out_shape=jax.ShapeDtypeStruct((N_TOK, D_FF), jnp.bfloat16),
        scratch_shapes=[pltpu.VMEM((D_MODEL, TN_FF), jnp.bfloat16),
                        pltpu.VMEM((D_MODEL, TN_FF), jnp.bfloat16)],
        compiler_params=_params(("arbitrary", "arbitrary")),
        name="ffn_up",
    )(hn, wg, wu)


def _ffn_down_kernel(a_ref, w_ref, h_ref, o_ref):
    o_ref[...] = h_ref[...] + jnp.dot(a_ref[...], w_ref[...],
                                      preferred_element_type=jnp.float32)


def ffn_down(act, w, h, l):
    return pl.pallas_call(
        _ffn_down_kernel,
        grid=(N_TOK // TM, D_MODEL // TN_DOWN),
        in_specs=[pl.BlockSpec((TM, D_FF), lambda i, j: (i, 0)),
                  pl.BlockSpec((None, D_FF, TN_DOWN), lambda i, j: (l, 0, j)),
                  pl.BlockSpec((TM, TN_DOWN), lambda i, j: (i, j))],
        out_specs=pl.BlockSpec((TM, TN_DOWN), lambda i, j: (i, j)),
        out_shape=jax.ShapeDtypeStruct((N_TOK, D_MODEL), jnp.float32),
        compiler_params=_params(("arbitrary", "arbitrary")),
        name="ffn_down",
    )(act, w, h)


def _final_norm_kernel(x_ref, g_ref, yp_ref, ys_ref):
    is_prompt = pl.program_id(0) < N_PROMPT_TILES
    pl.when(is_prompt)(lambda: _rms_rows(x_ref, g_ref, yp_ref, TM_OUT))
    pl.when(jnp.logical_not(is_prompt))(lambda: _rms_rows(x_ref, g_ref, ys_ref, TM_OUT))


def final_norm(x, g):
    return pl.pallas_call(
        _final_norm_kernel,
        grid=(N_TOK // TM_OUT,),
        in_specs=[pl.BlockSpec((TM_OUT, D_MODEL), lambda i: (i, 0)),
                  pl.BlockSpec((1, D_MODEL), lambda i: (0, 0))],
        out_specs=[pl.BlockSpec((TM_OUT, D_MODEL), lambda i: (jnp.minimum(i, N_PROMPT_TILES - 1), 0)),
                   pl.BlockSpec((N_SAMPLE, D_MODEL), lambda i: (0, 0))],
        out_shape=[jax.ShapeDtypeStruct((N_PROMPT, D_MODEL), jnp.float32),
                   jax.ShapeDtypeStruct((N_SAMPLE, D_MODEL), jnp.float32)],
        compiler_params=_params(("arbitrary",)),
        name="final_norm",
    )(x, g)


def _log_sigmoid(z):
    return jnp.minimum(z, 0.0) - jnp.log1p(jnp.exp(-jnp.abs(z)))


def _gelu(x):
    return 0.5 * x * (1.0 + lax.erf(x * SQRT_HALF))


def _layernorm(x, g, b):
    mu = jnp.mean(x, axis=-1, keepdims=True)
    xc = x - mu
    return xc * lax.rsqrt(jnp.mean(xc * xc, axis=-1, keepdims=True) + EPS) * g + b


def _dot(a, b):
    return jnp.dot(a, b, preferred_element_type=jnp.float32)


def _dot_nt(a, b):
    return lax.dot_general(a, b, (((1,), (1,)), ((), ())), preferred_element_type=jnp.float32)


def _dot_tn(a, b):
    return lax.dot_general(a, b, (((0,), (0,)), ((), ())), preferred_element_type=jnp.float32)


def _tril(n):
    r = lax.broadcasted_iota(jnp.int32, (n, n), 0)
    c = lax.broadcasted_iota(jnp.int32, (n, n), 1)
    return r >= c


def _prompt_mixer_kernel(proj_ref, pa_ref, wa2_ref, ba_ref, glag_ref, convw_ref, convb_ref, lng_ref, lnb_ref,
                         cmw_ref, cmb_ref, mix_ref, sout_ref, cout_ref, s_ref, zprev_ref):
    f32, bf16 = jnp.float32, jnp.bfloat16
    t_tile = pl.program_id(1)

    @pl.when(t_tile == 0)
    def _():
        s_ref[...] = jnp.zeros_like(s_ref)
        zprev_ref[...] = jnp.zeros_like(zprev_ref)

    causal = _tril(GLA_CHUNK)
    ones_tril = causal.astype(f32)
    q_scale = GLA_DK ** -0.5

    def gla_chunk(c, carry):
        rows = pl.ds(pl.multiple_of(c * GLA_CHUNK, GLA_CHUNK), GLA_CHUNK)
        z = _dot(pa_ref[rows, :], wa2_ref[...]) + ba_ref[...]
        la = _log_sigmoid(z) * (1.0 / GLA_TAU)
        b = jnp.dot(ones_tril, la, precision=lax.Precision.HIGHEST,
                    preferred_element_type=f32)
        g = b[GLA_CHUNK - 1:GLA_CHUNK, :]
        q = proj_ref[rows, OFF_Q:OFF_K].astype(f32) * q_scale
        k = proj_ref[rows, OFF_K:OFF_V].astype(f32)
        qt = (q * jnp.exp(b)).astype(bf16)
        kt = (k * jnp.exp(-b)).astype(bf16)
        kd = (k * jnp.exp(g - b)).astype(bf16)
        eg = jnp.exp(g)
        for h in range(GLA_HEADS):
            dk = slice(h * GLA_DK, (h + 1) * GLA_DK)
            dv = slice(h * GLA_DV, (h + 1) * GLA_DV)
            v_h = proj_ref[rows, OFF_V + h * GLA_DV:OFF_V + (h + 1) * GLA_DV]
            s_h = s_ref[h]
            a = jnp.where(causal, _dot_nt(qt[:, dk], kt[:, dk]), 0.0).astype(bf16)
            o = _dot(a, v_h) + _dot(qt[:, dk], s_h.astype(bf16))
            ds = _dot_tn(kd[:, dk], v_h)
            eg_rows = jnp.transpose(jnp.broadcast_to(eg[:, dk], (GLA_DK, GLA_DK)))
            s_ref[h] = jnp.concatenate([eg_rows, eg_rows], axis=1) * s_h + ds
            o = o * lax.rsqrt(jnp.mean(o * o, axis=-1, keepdims=True) + EPS) * glag_ref[:, dv]
            r_h = proj_ref[rows, OFF_R + h * GLA_DV:OFF_R + (h + 1) * GLA_DV].astype(f32)
            mix_ref[rows, dv] = (o * jax.nn.silu(r_h)).astype(bf16)
        return carry

    lax.fori_loop(0, TT // GLA_CHUNK, gla_chunk, 0, unroll=4)

    w_tril = _tril(CM_CHUNK)
    wm = [jnp.where(w_tril, cmw_ref[h], 0.0).astype(bf16) for h in range(CM_HEADS)]
    row_id = lax.broadcasted_iota(jnp.int32, (CM_CHUNK, D_CONV), 0)

    def cm_block(i, carry):
        rows = pl.ds(pl.multiple_of(i * CM_CHUNK, CM_CHUNK), CM_CHUNK)
        z = proj_ref[rows, OFF_CC:OFF_CH].astype(f32) * proj_ref[rows, OFF_CH:OFF_CU].astype(f32)
        prev = zprev_ref[...]
        p1 = prev[SUBLANES_V7X - 1:SUBLANES_V7X, :]
        p2 = prev[SUBLANES_V7X - 2:SUBLANES_V7X - 1, :]
        z1 = jnp.where(row_id == 0, p1, pltpu.roll(z, 1, 0))
        z2 = jnp.where(row_id == 0, p2, jnp.where(row_id == 1, p1, pltpu.roll(z, 2, 0)))
        conv = convb_ref[...] + convw_ref[0:1, :] * z2
        conv = conv + convw_ref[1:2, :] * z1
        conv = conv + convw_ref[2:3, :] * z
        cb = proj_ref[rows, OFF_CB:OFF_CC].astype(f32)
        mix_ref[rows, MIX_CONV:MIX_CM] = (cb * conv).astype(bf16)
        zprev_ref[...] = z[CM_CHUNK - SUBLANES_V7X:, :]
        cout_ref[0] = z[CM_CHUNK - (CONV_W - 1):, :]

        u = _gelu(proj_ref[rows, OFF_CU:OFF_CV].astype(f32))
        vv = _layernorm(_gelu(proj_ref[rows, OFF_CV:D_MAIN].astype(f32)), lng_ref[...], lnb_ref[...])
        vvb = vv.astype(bf16)
        for h in range(CM_HEADS):
            hd = slice(h * CM_HD, (h + 1) * CM_HD)
            zc = _dot(wm[h], vvb[:, hd]) + cmb_ref[:, hd]
            mix_ref[rows, MIX_CM + h * CM_HD:MIX_CM + (h + 1) * CM_HD] = (u[:, hd] * zc).astype(bf16)
        return carry

    lax.fori_loop(0, TT // CM_CHUNK, cm_block, 0, unroll=2)

    @pl.when(t_tile == pl.num_programs(1) - 1)
    def _():
        sout_ref[0] = s_ref[...]


def prompt_mixer(proj, pa, wa2, ba, glag, convw, convb, lng, lnb, cmw, cmb):
    const = lambda shape: pl.BlockSpec(shape, lambda b, t: (0,) * len(shape))
    tiles = SEQ // TT
    return pl.pallas_call(
        _prompt_mixer_kernel,
        grid=(BATCH, tiles),
        in_specs=[pl.BlockSpec((TT, D_MAIN), lambda b, t: (b * tiles + t, 0)),
                  pl.BlockSpec((TT, A_PAD), lambda b, t: (b * tiles + t, 0)),
                  const((A_PAD, GLA_HEADS * GLA_DK)), const((1, GLA_HEADS * GLA_DK)),
                  const((1, D_GLA)), const((CONV_W, D_CONV)), const((1, D_CONV)),
                  const((1, D_CM)), const((1, D_CM)),
                  const((CM_HEADS, CM_CHUNK, CM_CHUNK)), const((CM_CHUNK, D_CM))],
        out_specs=[pl.BlockSpec((TT, D_MODEL), lambda b, t: (b * tiles + t, 0)),
                   pl.BlockSpec((1, GLA_HEADS, GLA_DK, GLA_DV), lambda b, t: (b, 0, 0, 0)),
                   pl.BlockSpec((1, CONV_W - 1, D_CONV), lambda b, t: (b, 0, 0))],
        out_shape=[jax.ShapeDtypeStruct((N_PROMPT, D_MODEL), jnp.bfloat16),
                   jax.ShapeDtypeStruct((BATCH, GLA_HEADS, GLA_DK, GLA_DV), jnp.float32),
                   jax.ShapeDtypeStruct((BATCH, CONV_W - 1, D_CONV), jnp.float32)],
        scratch_shapes=[pltpu.VMEM((GLA_HEADS, GLA_DK, GLA_DV), jnp.float32),
                        pltpu.VMEM((SUBLANES_V7X, D_CONV), jnp.float32)],
        compiler_params=_params(("arbitrary", "arbitrary")),
        name="prompt_mixer",
    )(proj, pa, wa2, ba, glag, convw, convb, lng, lnb, cmw, cmb)


NB = 8
SLOT = SUBLANES_V7X


def _sample_mixer_kernel(proj_ref, pa_ref, sin_ref, _new_state_hbm, cbuf_ref, wa2_ref, ba_ref, glag_ref, convw_ref, convb_ref,
                         lng_ref, lnb_ref, cmw_ref, cmb_ref,
                         mix_ref, sout_ref, cout_ref, vv_ref,
                         qb_s, kb_s, vb_s, ob_s, oi_s, eg_s):
    f32, bf16 = jnp.float32, jnp.bfloat16
    step = pl.program_id(0)
    q_scale = GLA_DK ** -0.5
    trows = [slice(t * DEC_BATCH, (t + 1) * DEC_BATCH) for t in range(DEC_SEQ)]

    @pl.when(step == 0)
    def _():
        qb_s[...] = jnp.zeros_like(qb_s)
        kb_s[...] = jnp.zeros_like(kb_s)
        vb_s[...] = jnp.zeros_like(vb_s)

        cum, b = [], None
        for t in range(DEC_SEQ):
            z = _dot(pa_ref[trows[t], :], wa2_ref[...]) + ba_ref[...]
            la = _log_sigmoid(z) * (1.0 / GLA_TAU)
            b = la if b is None else b + la
            cum.append(b)
        g = cum[-1]
        eg_s[...] = jnp.exp(g)
        qt, kt, vs = [], [], []
        for t in range(DEC_SEQ):
            q = proj_ref[trows[t], OFF_Q:OFF_K].astype(f32) * q_scale
            k = proj_ref[trows[t], OFF_K:OFF_V].astype(f32)
            v = proj_ref[trows[t], OFF_V:OFF_R].astype(f32)
            qt.append(q * jnp.exp(cum[t]))
            kt.append(k * jnp.exp(-cum[t]))
            vs.append(v)
            seq_rows = pl.ds(t, DEC_BATCH, stride=SLOT)
            kd = k * jnp.exp(g - cum[t])
            for j in range(GLA_HEADS):
                qb_s[j, seq_rows, :] = qt[t][:, j * LANES_V7X:(j + 1) * LANES_V7X]
                kb_s[j, seq_rows, :] = kd[:, j * LANES_V7X:(j + 1) * LANES_V7X]
            for j in range(D_GLA // LANES_V7X):
                vb_s[j, seq_rows, :] = v[:, j * LANES_V7X:(j + 1) * LANES_V7X]
        for t in range(DEC_SEQ):
            heads = []
            for h in range(GLA_HEADS):
                dk = slice(h * GLA_DK, (h + 1) * GLA_DK)
                dv = slice(h * GLA_DV, (h + 1) * GLA_DV)
                acc = None
                for s in range(t + 1):
                    a_ts = jnp.sum(qt[t][:, dk] * kt[s][:, dk], axis=-1, keepdims=True)
                    term = a_ts * vs[s][:, dv]
                    acc = term if acc is None else acc + term
                heads.append(acc)
            oi_s[trows[t], :] = jnp.concatenate(heads, axis=1)

        zp = [cbuf_ref[0], cbuf_ref[1]]
        for t in range(DEC_SEQ):
            zp.append(proj_ref[trows[t], OFF_CC:OFF_CH].astype(f32)
                      * proj_ref[trows[t], OFF_CH:OFF_CU].astype(f32))
        for t in range(DEC_SEQ):
            conv = convb_ref[...] + convw_ref[0:1, :] * zp[t]
            conv = conv + convw_ref[1:2, :] * zp[t + 1]
            conv = conv + convw_ref[2:3, :] * zp[t + 2]
            cb = proj_ref[trows[t], OFF_CB:OFF_CC].astype(f32)
            mix_ref[trows[t], MIX_CONV:MIX_CM] = (cb * conv).astype(bf16)
        cout_ref[0] = zp[DEC_SEQ]
        cout_ref[1] = zp[DEC_SEQ + 1]

        vvs = []
        for t in range(DEC_SEQ):
            vv = _layernorm(_gelu(proj_ref[trows[t], OFF_CV:D_MAIN].astype(f32)), lng_ref[...], lnb_ref[...])
            vv_ref[trows[t], :] = vv
            vvs.append(vv)
        for t in range(DEC_SEQ):
            zc = cmb_ref[t:t + 1, :]
            for s in range(t + 1):
                w_ts = cmw_ref[t * DEC_SEQ + s:t * DEC_SEQ + s + 1, :]
                zc = zc + w_ts * vvs[s]
            u = _gelu(proj_ref[trows[t], OFF_CU:OFF_CV].astype(f32))
            mix_ref[trows[t], MIX_CM:] = (u * zc).astype(bf16)

    def seq_body(i, carry):
        seq = step * NB + i
        slot = pl.ds(pl.multiple_of(seq * SLOT, SLOT), SLOT)
        eg_row = eg_s[pl.ds(seq, 1), :]
        for h in range(GLA_HEADS):
            dk = slice(h * GLA_DK, (h + 1) * GLA_DK)
            dv = slice(h * GLA_DV, (h + 1) * GLA_DV)
            s0 = sin_ref[i, h]
            o_seq = _dot(qb_s[h, slot, :].astype(bf16), s0.astype(bf16))
            ob_s[2 * h, slot, :] = o_seq[:, :LANES_V7X]
            ob_s[2 * h + 1, slot, :] = o_seq[:, LANES_V7X:]
            v_seq = jnp.concatenate([vb_s[2 * h, slot, :], vb_s[2 * h + 1, slot, :]], axis=1)
            ds = _dot_tn(kb_s[h, slot, :].astype(bf16), v_seq.astype(bf16))
            eg_rows = jnp.transpose(jnp.broadcast_to(eg_row[:, dk], (GLA_DK, GLA_DK)))
            sout_ref[i, h] = jnp.concatenate([eg_rows, eg_rows], axis=1) * s0 + ds
        return carry

    lax.fori_loop(0, NB, seq_body, 0, unroll=4)

    @pl.when(step == pl.num_programs(0) - 1)
    def _():
        for t in range(DEC_SEQ):
            seq_rows = pl.ds(t, DEC_BATCH, stride=SLOT)
            o_t = oi_s[trows[t], :] + jnp.concatenate(
                [ob_s[j, seq_rows, :] for j in range(D_GLA // LANES_V7X)], axis=1)
            for h in range(GLA_HEADS):
                dv = slice(h * GLA_DV, (h + 1) * GLA_DV)
                o = o_t[:, dv]
                o = o * lax.rsqrt(jnp.mean(o * o, axis=-1, keepdims=True) + EPS) * glag_ref[:, dv]
                r_h = proj_ref[trows[t], OFF_R + h * GLA_DV:OFF_R + (h + 1) * GLA_DV].astype(f32)
                mix_ref[trows[t], dv] = (o * jax.nn.silu(r_h)).astype(bf16)


def sample_mixer(proj, pa, state, new_state, cbuf, wa2, ba, glag, convw, convb, lng, lnb, cmw, cmb, l):
    const = lambda shape: pl.BlockSpec(shape, lambda j: (0,) * len(shape))
    seq_rows = DEC_BATCH * SLOT
    state_block = pl.BlockSpec((None, NB, GLA_HEADS, GLA_DK, GLA_DV), lambda j: (l, j, 0, 0, 0))
    return pl.pallas_call(
        _sample_mixer_kernel,
        grid=(DEC_BATCH // NB,),
        in_specs=[pl.BlockSpec((N_SAMPLE, D_MAIN), lambda j: (N_PROMPT // N_SAMPLE, 0)),
                  pl.BlockSpec((N_SAMPLE, A_PAD), lambda j: (N_PROMPT // N_SAMPLE, 0)),
                  state_block,
                  pl.BlockSpec(memory_space=pl.ANY),
                  const((CONV_W - 1, DEC_BATCH, D_CONV)),
                  const((A_PAD, GLA_HEADS * GLA_DK)), const((1, GLA_HEADS * GLA_DK)),
                  const((1, D_GLA)), const((CONV_W, D_CONV)), const((1, D_CONV)),
                  const((1, D_CM)), const((1, D_CM)),
                  const((DEC_SEQ * DEC_SEQ, D_CM)), const((CM_CHUNK, D_CM))],
        out_specs=[const((N_SAMPLE, D_MODEL)),
                   state_block,
                   const((CONV_W - 1, DEC_BATCH, D_CONV)),
                   const((N_SAMPLE, D_CM))],
        out_shape=[jax.ShapeDtypeStruct((N_SAMPLE, D_MODEL), jnp.bfloat16),
                   jax.ShapeDtypeStruct((DEPTH, DEC_BATCH, GLA_HEADS, GLA_DK, GLA_DV), jnp.float32),
                   jax.ShapeDtypeStruct((CONV_W - 1, DEC_BATCH, D_CONV), jnp.float32),
                   jax.ShapeDtypeStruct((N_SAMPLE, D_CM), jnp.float32)],
        scratch_shapes=[pltpu.VMEM((GLA_HEADS, seq_rows, LANES_V7X), jnp.float32),
                        pltpu.VMEM((GLA_HEADS, seq_rows, LANES_V7X), jnp.float32),
                        pltpu.VMEM((D_GLA // LANES_V7X, seq_rows, LANES_V7X), jnp.float32),
                        pltpu.VMEM((D_GLA // LANES_V7X, seq_rows, LANES_V7X), jnp.float32),
                        pltpu.VMEM((N_SAMPLE, D_GLA), jnp.float32),
                        pltpu.VMEM((DEC_BATCH, GLA_HEADS * GLA_DK), jnp.float32)],
        input_output_aliases={3: 1},
        compiler_params=_params(("arbitrary",)),
        name="sample_mixer",
    )(proj, pa, state, new_state, cbuf, wa2, ba, glag, convw, convb, lng, lnb, cmw, cmb)


def _time_major(x):
    return jnp.swapaxes(x, 0, 1).reshape(N_SAMPLE, x.shape[-1])


def _batch_major(x):
    return jnp.swapaxes(x.reshape(DEC_SEQ, DEC_BATCH, x.shape[-1]), 0, 1)


def _pack_w_in(w_in):
    a0 = OFF_CB
    a1 = a0 + GLA_RANK
    main = jnp.concatenate([w_in[..., :a0], w_in[..., a1:]], axis=-1).astype(jnp.bfloat16)
    gate = jnp.pad(w_in[..., a0:a1], ((0, 0), (0, 0), (0, A_PAD - GLA_RANK))).astype(jnp.bfloat16)
    return main, gate


def kernel(x_prompt, x_sample, state_gla, state_conv, norm1_g, w_in, w_a2, b_a, gla_g, conv_w, conv_b,
           cm_ln_g, cm_ln_b, cm_ws, cm_bs, w_out, norm2_g, w_gate, w_up, w_down, final_g):
    bf = jnp.bfloat16
    w_main, w_gate_rank = _pack_w_in(w_in)
    w_out_b, w_down_b = w_out.astype(bf), w_down.astype(bf)
    wa2_p = jnp.pad(w_a2, ((0, 0), (0, A_PAD - GLA_RANK), (0, 0))).astype(bf)
    cmb_rows = jnp.repeat(jnp.swapaxes(cm_bs, 1, 2), CM_HD, axis=2)
    cmw_small = jnp.repeat(jnp.transpose(cm_ws[:, :, :DEC_SEQ, :DEC_SEQ], (0, 2, 3, 1))
                           .reshape(DEPTH, DEC_SEQ * DEC_SEQ, CM_HEADS), CM_HD, axis=2)
    cbuf_tm = jnp.swapaxes(state_conv, 1, 2)
    row = lambda a: a.reshape(1, -1)

    xp, xs, xs_block = x_prompt.reshape(N_PROMPT, D_MODEL), _time_major(x_sample), 0
    gla_p, conv_p, conv_s, cmv_s = [], [], [], []
    gla_s = jnp.zeros(state_gla.shape, state_gla.dtype)
    for l in range(DEPTH):
        lw = (wa2_p[l], row(b_a[l]), row(gla_g[l]), conv_w[l], row(conv_b[l]), row(cm_ln_g[l]), row(cm_ln_b[l]))
        proj, pa = in_proj(xp, xs, xs_block, row(norm1_g[l]), w_main, w_gate_rank, l)
        mix_p, Sp, cp = prompt_mixer(proj, pa, *lw, cm_ws[l], cmb_rows[l])
        mix_s, gla_s, cs, vs = sample_mixer(proj, pa, state_gla, gla_s, cbuf_tm[l], *lw,
                                            cmw_small[l], cmb_rows[l], l)
        h, hn = out_proj(mix_p, mix_s, w_out_b, xp, xs, xs_block, row(norm2_g[l]), l)
        act = ffn_up(hn, w_gate, w_up, l)
        x = ffn_down(act, w_down_b, h, l)
        xp, xs, xs_block = x, x, N_PROMPT_TILES
        gla_p.append(Sp); conv_p.append(cp)
        conv_s.append(jnp.swapaxes(cs, 0, 1)); cmv_s.append(_batch_major(vs))
    y_p, y_s = final_norm(x, row(final_g))
    y_prompt = y_p.reshape(BATCH, SEQ, D_MODEL)
    y_sample = _batch_major(y_s)
    return (y_prompt, y_sample, jnp.stack(gla_p), jnp.stack(conv_p), gla_s,
            jnp.stack(conv_s), jnp.stack(cmv_s))
```

```python
import math

import jax
import jax.numpy as jnp
from jax import lax
from jax.experimental import pallas as pl
from jax.experimental.pallas import tpu as pltpu

D_MODEL = 2048
BATCH = 4
SEQ = 2048
DEPTH = 4
DEC_BATCH = 128
DEC_SEQ = 4
D_GLA = 1024
GLA_HEADS = 4
GLA_DK = 128
GLA_DV = 256
GLA_RANK = 16
GLA_TAU = 16.0
GLA_CHUNK = 64
D_CONV = 512
CONV_W = 3
D_CM = 512
CM_HEADS = 4
CM_HD = 128
CM_CHUNK = 128
D_FF = 5632
EPS = 1e-6
SQRT_HALF = math.sqrt(0.5)

N_PROMPT = BATCH * SEQ
N_SAMPLE = DEC_BATCH * DEC_SEQ
N_TOK = N_PROMPT + N_SAMPLE

OFF_Q, OFF_K, OFF_V, OFF_R = 0, 512, 1024, 2048
OFF_CB, OFF_CC, OFF_CH, OFF_CU, OFF_CV = 3072, 3584, 4096, 4608, 5120
D_MAIN = 5632
A_PAD = 128
MIX_CONV, MIX_CM = D_GLA, D_GLA + D_CONV

LANES_V7X = 128
SUBLANES_V7X = 8
VMEM_LIMIT_V7X = 56 * 1024 * 1024

TM = 1088
TM_OUT = 512
TN_IN = D_MAIN // 2
TN_FF = 512
TN_DOWN = 512
NORM_ROWS = 32
TT = 512
GLA_GROUP = 256


def _params(sem):
    return pltpu.CompilerParams(dimension_semantics=sem, vmem_limit_bytes=VMEM_LIMIT_V7X)


def _rms_rows(x_ref, g_ref, o_ref, rows):
    g = g_ref[...]

    def body(c, carry):
        sl = pl.ds(pl.multiple_of(c * NORM_ROWS, NORM_ROWS), NORM_ROWS)
        xf = x_ref[sl, :]
        ms = jnp.mean(xf * xf, axis=-1, keepdims=True)
        o_ref[sl, :] = (xf * lax.rsqrt(ms + EPS) * g).astype(o_ref.dtype)
        return carry

    lax.fori_loop(0, rows // NORM_ROWS, body, 0)


def _rms_rows_unrolled(load_rows, g, o_ref, rows):
    for c in range(rows // NORM_ROWS):
        sl = slice(c * NORM_ROWS, (c + 1) * NORM_ROWS)
        xf = load_rows(sl)
        ms = jnp.mean(xf * xf, axis=-1, keepdims=True)
        o_ref[sl, :] = (xf * lax.rsqrt(ms + EPS) * g).astype(o_ref.dtype)


def _row_tile_specs(sample_block, index):
    prompt = pl.BlockSpec((TM_OUT, D_MODEL),
                          lambda *ids: (jnp.minimum(index(*ids), N_PROMPT_TILES - 1), 0))
    sample = pl.BlockSpec((TM_OUT, D_MODEL), lambda *ids: (sample_block, 0),
                          pipeline_mode=pl.Buffered(1))
    return prompt, sample


N_PROMPT_TILES = N_PROMPT // TM_OUT
N_ROW_TILES = N_TOK // TM_OUT


def _in_proj_kernel(x0_ref, xp_ref, xs_ref, g_ref, w_ref, wa_ref, o_ref, oa_ref, xn_a, xn_b):
    j, i = pl.program_id(0), pl.program_id(1)
    g = g_ref[...]

    @pl.when((i == 0) & (j == 0))
    def _():
        _rms_rows(x0_ref, g_ref, xn_a, TM_OUT)

    next_is_sample = i == N_PROMPT_TILES - 1

    def load_next(sl):
        return jnp.where(next_is_sample, xs_ref[sl, :], xp_ref[sl, :])

    def step(cur, nxt):
        _rms_rows_unrolled(load_next, g, nxt, TM_OUT)
        o_ref[...] = jnp.dot(cur[...], w_ref[...],
                             preferred_element_type=jnp.float32).astype(o_ref.dtype)

        @pl.when(j == 0)
        def _():
            oa_ref[...] = jnp.dot(cur[...], wa_ref[...],
                                  preferred_element_type=jnp.float32).astype(oa_ref.dtype)

    even = (j * N_ROW_TILES + i) % 2 == 0
    pl.when(even)(lambda: step(xn_a, xn_b))
    pl.when(jnp.logical_not(even))(lambda: step(xn_b, xn_a))


def in_proj(xp, xs, xs_block, g, w, wa, l):
    nxt = lambda j, i: (i + 1) % N_ROW_TILES
    xp_spec, xs_spec = _row_tile_specs(xs_block, nxt)
    return pl.pallas_call(
        _in_proj_kernel,
        grid=(D_MAIN // TN_IN, N_ROW_TILES),
        in_specs=[pl.BlockSpec((TM_OUT, D_MODEL), lambda j, i: (0, 0), pipeline_mode=pl.Buffered(1)),
                  xp_spec, xs_spec,
                  pl.BlockSpec((1, D_MODEL), lambda j, i: (0, 0)),
                  pl.BlockSpec((None, D_MODEL, TN_IN), lambda j, i: (l, 0, j)),
                  pl.BlockSpec((None, D_MODEL, A_PAD), lambda j, i: (l, 0, 0))],
        out_specs=[pl.BlockSpec((TM_OUT, TN_IN), lambda j, i: (i, j)),
                   pl.BlockSpec((TM_OUT, A_PAD), lambda j, i: (jnp.where(j == 0, i, N_ROW_TILES - 1), 0))],
        out_shape=[jax.ShapeDtypeStruct((N_TOK, D_MAIN), jnp.bfloat16),
                   jax.ShapeDtypeStruct((N_TOK, A_PAD), jnp.bfloat16)],
        scratch_shapes=[pltpu.VMEM((TM_OUT, D_MODEL), jnp.bfloat16),
                        pltpu.VMEM((TM_OUT, D_MODEL), jnp.bfloat16)],
        compiler_params=_params(("arbitrary", "arbitrary")),
        name="in_proj",
    )(xp, xp, xs, g, w, wa)


def _out_proj_kernel(mixp_ref, mixs_ref, w_ref, xp_ref, xs_ref, g_ref, h_ref, hn_ref, hs_a, hs_b):
    i = pl.program_id(0)
    g = g_ref[...]
    is_sample = i == N_PROMPT_TILES

    def matmul(dst):
        mix = jnp.where(is_sample, mixs_ref[...], mixp_ref[...])
        x = jnp.where(is_sample, xs_ref[...], xp_ref[...])
        dst[...] = x + jnp.dot(mix, w_ref[...], preferred_element_type=jnp.float32)

    def finish(src):
        def load(sl):
            rows = src[sl, :]
            h_ref[sl, :] = rows
            return rows
        _rms_rows_unrolled(load, g, hn_ref, TM_OUT)

    odd = i % 2 == 1
    last = i == N_ROW_TILES

    @pl.when(i == 0)
    def _():
        matmul(hs_a)

    @pl.when(odd & jnp.logical_not(last))
    def _():
        matmul(hs_b)
        finish(hs_a)

    @pl.when(jnp.logical_not(odd) & (i > 0))
    def _():
        matmul(hs_a)
        finish(hs_b)

    @pl.when(last)
    def _():
        finish(hs_a)


def out_proj(mix_p, mix_s, w, xp, xs, xs_block, g, l):
    assert N_ROW_TILES % 2 == 1
    cur = lambda i: jnp.minimum(i, N_ROW_TILES - 1)
    xp_spec, xs_spec = _row_tile_specs(xs_block, cur)
    done = lambda i: (jnp.maximum(i - 1, 0), 0)
    return pl.pallas_call(
        _out_proj_kernel,
        grid=(N_ROW_TILES + 1,),
        in_specs=[pl.BlockSpec((TM_OUT, D_MODEL), lambda i: (jnp.minimum(i, N_PROMPT_TILES - 1), 0)),
                  pl.BlockSpec((N_SAMPLE, D_MODEL), lambda i: (0, 0), pipeline_mode=pl.Buffered(1)),
                  pl.BlockSpec((None, D_MODEL, D_MODEL), lambda i: (l, 0, 0), pipeline_mode=pl.Buffered(1)),
                  xp_spec, xs_spec,
                  pl.BlockSpec((1, D_MODEL), lambda i: (0, 0))],
        out_specs=[pl.BlockSpec((TM_OUT, D_MODEL), done),
                   pl.BlockSpec((TM_OUT, D_MODEL), done)],
        out_shape=[jax.ShapeDtypeStruct((N_TOK, D_MODEL), jnp.float32),
                   jax.ShapeDtypeStruct((N_TOK, D_MODEL), jnp.bfloat16)],
        scratch_shapes=[pltpu.VMEM((TM_OUT, D_MODEL), jnp.float32),
                        pltpu.VMEM((TM_OUT, D_MODEL), jnp.float32)],
        compiler_params=_params(("arbitrary",)),
        name="out_proj",
    )(mix_p, mix_s, w, xp, xs, g)


def _ffn_up_kernel(hn_ref, wg_ref, wu_ref, o_ref, wgb_ref, wub_ref):
    @pl.when(pl.program_id(1) == 0)
    def _():
        wgb_ref[...] = wg_ref[...].astype(jnp.bfloat16)
        wub_ref[...] = wu_ref[...].astype(jnp.bfloat16)

    hn = hn_ref[...]
    gate = jnp.dot(hn, wgb_ref[...], preferred_element_type=jnp.float32)
    up = jnp.dot(hn, wub_ref[...], preferred_element_type=jnp.float32)
    o_ref[...] = (jax.nn.silu(gate) * up).astype(o_ref.dtype)


def ffn_up(hn, wg, wu, l):
    return pl.pallas_call(
        _ffn_up_kernel,
        grid=(D_FF // TN_FF, N_TOK // TM),
        in_specs=[pl.BlockSpec((TM, D_MODEL), lambda j, i: (i, 0)),
                  pl.BlockSpec((None, D_MODEL, TN_FF), lambda j, i: (l, 0, j)),
                  pl.BlockSpec((None, D_MODEL, TN_FF), lambda j, i: (l, 0, j))],
        out_specs=pl.BlockSpec((TM, TN_FF), lambda j, i: (i, j)),
        out_shape=jax.ShapeDtypeStruct((N_TOK, D_FF), jnp.bfloat16),
        scratch_shapes=[pltpu.VMEM((D_MODEL, TN_FF), jnp.bfloat16),
                        pltpu.VMEM((D_MODEL, TN_FF), jnp.bfloat16)],
        compiler_params=_params(("arbitrary", "arbitrary")),
        name="ffn_up",
    )(hn, wg, wu)


def _ffn_down_kernel(a_ref, w_ref, h_ref, o_ref):
    o_ref[...] = h_ref[...] + jnp.dot(a_ref[...], w_ref[...],
                                      preferred_element_type=jnp.float32)


def ffn_down(act, w, h, l):
    return pl.pallas_call(
        _ffn_down_kernel,
        grid=(N_TOK // TM, D_MODEL // TN_DOWN),
        in_specs=[pl.BlockSpec((TM, D_FF), lambda i, j: (i, 0)),
                  pl.BlockSpec((None, D_FF, TN_DOWN), lambda i, j: (l, 0, j)),
                  pl.BlockSpec((TM, TN_DOWN), lambda i, j: (i, j))],
        out_specs=pl.BlockSpec((TM, TN_DOWN), lambda i, j: (i, j)),
        out_shape=jax.ShapeDtypeStruct((N_TOK, D_MODEL), jnp.float32),
        compiler_params=_params(("arbitrary", "arbitrary")),
        name="ffn_down",
    )(act, w, h)


def _final_norm_kernel(x_ref, g_ref, yp_ref, ys_ref):
    is_prompt = pl.program_id(0) < N_PROMPT_TILES
    pl.when(is_prompt)(lambda: _rms_rows(x_ref, g_ref, yp_ref, TM_OUT))
    pl.when(jnp.logical_not(is_prompt))(lambda: _rms_rows(x_ref, g_ref, ys_ref, TM_OUT))


def final_norm(x, g):
    return pl.pallas_call(
        _final_norm_kernel,
        grid=(N_TOK // TM_OUT,),
        in_specs=[pl.BlockSpec((TM_OUT, D_MODEL), lambda i: (i, 0)),
                  pl.BlockSpec((1, D_MODEL), lambda i: (0, 0))],
        out_specs=[pl.BlockSpec((TM_OUT, D_MODEL), lambda i: (jnp.minimum(i, N_PROMPT_TILES - 1), 0)),
                   pl.BlockSpec((N_SAMPLE, D_MODEL), lambda i: (0, 0))],
        out_shape=[jax.ShapeDtypeStruct((N_PROMPT, D_MODEL), jnp.float32),
                   jax.ShapeDtypeStruct((N_SAMPLE, D_MODEL), jnp.float32)],
        compiler_params=_params(("arbitrary",)),
        name="final_norm",
    )(x, g)


def _log_sigmoid(z):
    return jnp.minimum(z, 0.0) - jnp.log1p(jnp.exp(-jnp.abs(z)))


def _gelu(x):
    return 0.5 * x * (1.0 + lax.erf(x * SQRT_HALF))


def _layernorm(x, g, b):
    mu = jnp.mean(x, axis=-1, keepdims=True)
    xc = x - mu
    return xc * lax.rsqrt(jnp.mean(xc * xc, axis=-1, keepdims=True) + EPS) * g + b


def _dot(a, b):
    return jnp.dot(a, b, preferred_element_type=jnp.float32)


def _dot_nt(a, b):
    return lax.dot_general(a, b, (((1,), (1,)), ((), ())), preferred_element_type=jnp.float32)


def _dot_tn(a, b):
    return lax.dot_general(a, b, (((0,), (0,)), ((), ())), preferred_element_type=jnp.float32)


def _tril(n):
    r = lax.broadcasted_iota(jnp.int32, (n, n), 0)
    c = lax.broadcasted_iota(jnp.int32, (n, n), 1)
    return r >= c


def _prompt_mixer_kernel(proj_ref, pa_ref, wa2_ref, ba_ref, glag_ref, convw_ref, convb_ref, lng_ref, lnb_ref,
                         cmw_ref, cmb_ref, mix_ref, sout_ref, cout_ref, s_ref, zprev_ref,
                         qt_s, kd_s, b_s, oi_s):
    f32, bf16 = jnp.float32, jnp.bfloat16
    t_tile = pl.program_id(1)

    @pl.when(t_tile == 0)
    def _():
        s_ref[...] = jnp.zeros_like(s_ref)
        zprev_ref[...] = jnp.zeros_like(zprev_ref)

    q_scale = GLA_DK ** -0.5
    n_chunks = TT // GLA_CHUNK
    hk = GLA_HEADS * GLA_DK
    head_dk = [slice(h * GLA_DK, (h + 1) * GLA_DK) for h in range(GLA_HEADS)]
    head_dv = [slice(h * GLA_DV, (h + 1) * GLA_DV) for h in range(GLA_HEADS)]

    r = lax.broadcasted_iota(jnp.int32, (GLA_GROUP, GLA_GROUP), 0)
    c = lax.broadcasted_iota(jnp.int32, (GLA_GROUP, GLA_GROUP), 1)
    chunk_shift = GLA_CHUNK.bit_length() - 1
    causal = ((r >> chunk_shift) == (c >> chunk_shift)) & (r >= c)
    causal_ones = causal.astype(bf16)
    group_chunks = GLA_GROUP // GLA_CHUNK

    for grp in range(TT // GLA_GROUP):
        rows = slice(grp * GLA_GROUP, (grp + 1) * GLA_GROUP)
        z = _dot(pa_ref[rows, :], wa2_ref[...]) + ba_ref[...]
        la = _log_sigmoid(z) * (1.0 / GLA_TAU)
        hi = la.astype(bf16)
        rest = la - hi.astype(f32)
        mid = rest.astype(bf16)
        lo = (rest - mid.astype(f32)).astype(bf16)
        cum = _dot(causal_ones, jnp.concatenate([hi, mid, lo], axis=1))
        b = (cum[:, :hk] + cum[:, hk:2 * hk]) + cum[:, 2 * hk:]
        b_s[rows, :] = b
        g = jnp.broadcast_to(b.reshape(group_chunks, GLA_CHUNK, hk)[:, GLA_CHUNK - 1:, :],
                             (group_chunks, GLA_CHUNK, hk)).reshape(GLA_GROUP, hk)
        q = proj_ref[rows, OFF_Q:OFF_K].astype(f32) * q_scale
        k = proj_ref[rows, OFF_K:OFF_V].astype(f32)
        qt = (q * jnp.exp(b)).astype(bf16)
        kt = (k * jnp.exp(-b)).astype(bf16)
        qt_s[rows, :] = qt
        kd_s[rows, :] = (k * jnp.exp(g - b)).astype(bf16)
        for h in range(GLA_HEADS):
            a = jnp.where(causal, _dot_nt(qt[:, head_dk[h]], kt[:, head_dk[h]]), 0.0).astype(bf16)
            oi_s[rows, head_dv[h]] = _dot(a, proj_ref[rows, OFF_V + h * GLA_DV:OFF_V + (h + 1) * GLA_DV])

    def chunk_state(ci, carry):
        rows = pl.ds(pl.multiple_of(ci * GLA_CHUNK, GLA_CHUNK), GLA_CHUNK)
        eg = jnp.exp(b_s[pl.ds(ci * GLA_CHUNK + (GLA_CHUNK - 1), 1), :])
        for h in range(GLA_HEADS):
            dk, dv = head_dk[h], head_dv[h]
            v_h = proj_ref[rows, OFF_V + h * GLA_DV:OFF_V + (h + 1) * GLA_DV]
            s_h = s_ref[h]
            o = oi_s[rows, dv] + _dot(qt_s[rows, dk], s_h.astype(bf16))
            ds = _dot_tn(kd_s[rows, dk], v_h)
            eg_rows = jnp.transpose(jnp.broadcast_to(eg[:, dk], (GLA_DK, GLA_DK)))
            s_ref[h] = jnp.concatenate([eg_rows, eg_rows], axis=1) * s_h + ds
            o = o * lax.rsqrt(jnp.mean(o * o, axis=-1, keepdims=True) + EPS) * glag_ref[:, dv]
            r_h = proj_ref[rows, OFF_R + h * GLA_DV:OFF_R + (h + 1) * GLA_DV].astype(f32)
            mix_ref[rows, dv] = (o * jax.nn.silu(r_h)).astype(bf16)
        return carry

    lax.fori_loop(0, n_chunks, chunk_state, 0, unroll=4)

    w_tril = _tril(CM_CHUNK)
    wm = [jnp.where(w_tril, cmw_ref[h], 0.0).astype(bf16) for h in range(CM_HEADS)]
    row_id = lax.broadcasted_iota(jnp.int32, (CM_CHUNK, D_CONV), 0)

    def cm_block(i, carry):
        rows = pl.ds(pl.multiple_of(i * CM_CHUNK, CM_CHUNK), CM_CHUNK)
        z = proj_ref[rows, OFF_CC:OFF_CH].astype(f32) * proj_ref[rows, OFF_CH:OFF_CU].astype(f32)
        prev = zprev_ref[...]
        p1 = prev[SUBLANES_V7X - 1:SUBLANES_V7X, :]
        p2 = prev[SUBLANES_V7X - 2:SUBLANES_V7X - 1, :]
        z1 = jnp.where(row_id == 0, p1, pltpu.roll(z, 1, 0))
        z2 = jnp.where(row_id == 0, p2, jnp.where(row_id == 1, p1, pltpu.roll(z, 2, 0)))
        conv = convb_ref[...] + convw_ref[0:1, :] * z2
        conv = conv + convw_ref[1:2, :] * z1
        conv = conv + convw_ref[2:3, :] * z
        cb = proj_ref[rows, OFF_CB:OFF_CC].astype(f32)
        mix_ref[rows, MIX_CONV:MIX_CM] = (cb * conv).astype(bf16)
        zprev_ref[...] = z[CM_CHUNK - SUBLANES_V7X:, :]
        cout_ref[0] = z[CM_CHUNK - (CONV_W - 1):, :]

        u = _gelu(proj_ref[rows, OFF_CU:OFF_CV].astype(f32))
        vv = _layernorm(_gelu(proj_ref[rows, OFF_CV:D_MAIN].astype(f32)), lng_ref[...], lnb_ref[...])
        vvb = vv.astype(bf16)
        for h in range(CM_HEADS):
            hd = slice(h * CM_HD, (h + 1) * CM_HD)
            zc = _dot(wm[h], vvb[:, hd]) + cmb_ref[:, hd]
            mix_ref[rows, MIX_CM + h * CM_HD:MIX_CM + (h + 1) * CM_HD] = (u[:, hd] * zc).astype(bf16)
        return carry

    lax.fori_loop(0, TT // CM_CHUNK, cm_block, 0, unroll=2)

    @pl.when(t_tile == pl.num_programs(1) - 1)
    def _():
        sout_ref[0] = s_ref[...]


def prompt_mixer(proj, pa, wa2, ba, glag, convw, convb, lng, lnb, cmw, cmb):
    const = lambda shape: pl.BlockSpec(shape, lambda b, t: (0,) * len(shape))
    tiles = SEQ // TT
    return pl.pallas_call(
        _prompt_mixer_kernel,
        grid=(BATCH, tiles),
        in_specs=[pl.BlockSpec((TT, D_MAIN), lambda b, t: (b * tiles + t, 0)),
                  pl.BlockSpec((TT, A_PAD), lambda b, t: (b * tiles + t, 0)),
                  const((A_PAD, GLA_HEADS * GLA_DK)), const((1, GLA_HEADS * GLA_DK)),
                  const((1, D_GLA)), const((CONV_W, D_CONV)), const((1, D_CONV)),
                  const((1, D_CM)), const((1, D_CM)),
                  const((CM_HEADS, CM_CHUNK, CM_CHUNK)), const((CM_CHUNK, D_CM))],
        out_specs=[pl.BlockSpec((TT, D_MODEL), lambda b, t: (b * tiles + t, 0)),
                   pl.BlockSpec((1, GLA_HEADS, GLA_DK, GLA_DV), lambda b, t: (b, 0, 0, 0)),
                   pl.BlockSpec((1, CONV_W - 1, D_CONV), lambda b, t: (b, 0, 0))],
        out_shape=[jax.ShapeDtypeStruct((N_PROMPT, D_MODEL), jnp.bfloat16),
                   jax.ShapeDtypeStruct((BATCH, GLA_HEADS, GLA_DK, GLA_DV), jnp.float32),
                   jax.ShapeDtypeStruct((BATCH, CONV_W - 1, D_CONV), jnp.float32)],
        scratch_shapes=[pltpu.VMEM((GLA_HEADS, GLA_DK, GLA_DV), jnp.float32),
                        pltpu.VMEM((SUBLANES_V7X, D_CONV), jnp.float32),
                        pltpu.VMEM((TT, GLA_HEADS * GLA_DK), jnp.bfloat16),
                        pltpu.VMEM((TT, GLA_HEADS * GLA_DK), jnp.bfloat16),
                        pltpu.VMEM((TT, GLA_HEADS * GLA_DK), jnp.float32),
                        pltpu.VMEM((TT, D_GLA), jnp.float32)],
        compiler_params=_params(("arbitrary", "arbitrary")),
        name="prompt_mixer",
    )(proj, pa, wa2, ba, glag, convw, convb, lng, lnb, cmw, cmb)


NB = 8
SLOT = SUBLANES_V7X


def _sample_mixer_kernel(proj_ref, pa_ref, sin_ref, _new_state_hbm, cbuf_ref, wa2_ref, ba_ref, glag_ref, convw_ref, convb_ref,
                         lng_ref, lnb_ref, cmw_ref, cmb_ref,
                         mix_ref, sout_ref, cout_ref, vv_ref,
                         qb_s, kb_s, vb_s, ob_s, oi_s, eg_s):
    f32, bf16 = jnp.float32, jnp.bfloat16
    step = pl.program_id(0)
    q_scale = GLA_DK ** -0.5
    trows = [slice(t * DEC_BATCH, (t + 1) * DEC_BATCH) for t in range(DEC_SEQ)]

    @pl.when(step == 0)
    def _():
        qb_s[...] = jnp.zeros_like(qb_s)
        kb_s[...] = jnp.zeros_like(kb_s)
        vb_s[...] = jnp.zeros_like(vb_s)

        cum, b = [], None
        for t in range(DEC_SEQ):
            z = _dot(pa_ref[trows[t], :], wa2_ref[...]) + ba_ref[...]
            la = _log_sigmoid(z) * (1.0 / GLA_TAU)
            b = la if b is None else b + la
            cum.append(b)
        g = cum[-1]
        eg_s[...] = jnp.exp(g)
        qt, kt, vs = [], [], []
        for t in range(DEC_SEQ):
            q = proj_ref[trows[t], OFF_Q:OFF_K].astype(f32) * q_scale
            k = proj_ref[trows[t], OFF_K:OFF_V].astype(f32)
            v = proj_ref[trows[t], OFF_V:OFF_R].astype(f32)
            qt.append(q * jnp.exp(cum[t]))
            kt.append(k * jnp.exp(-cum[t]))
            vs.append(v)
            seq_rows = pl.ds(t, DEC_BATCH, stride=SLOT)
            kd = k * jnp.exp(g - cum[t])
            for j in range(GLA_HEADS):
                qb_s[j, seq_rows, :] = qt[t][:, j * LANES_V7X:(j + 1) * LANES_V7X]
                kb_s[j, seq_rows, :] = kd[:, j * LANES_V7X:(j + 1) * LANES_V7X]
            for j in range(D_GLA // LANES_V7X):
                vb_s[j, seq_rows, :] = v[:, j * LANES_V7X:(j + 1) * LANES_V7X]
        for t in range(DEC_SEQ):
            heads = []
            for h in range(GLA_HEADS):
                dk = slice(h * GLA_DK, (h + 1) * GLA_DK)
                dv = slice(h * GLA_DV, (h + 1) * GLA_DV)
                acc = None
                for s in range(t + 1):
                    a_ts = jnp.sum(qt[t][:, dk] * kt[s][:, dk], axis=-1, keepdims=True)
                    term = a_ts * vs[s][:, dv]
                    acc = term if acc is None else acc + term
                heads.append(acc)
            oi_s[trows[t], :] = jnp.concatenate(heads, axis=1)

        zp = [cbuf_ref[0], cbuf_ref[1]]
        for t in range(DEC_SEQ):
            zp.append(proj_ref[trows[t], OFF_CC:OFF_CH].astype(f32)
                      * proj_ref[trows[t], OFF_CH:OFF_CU].astype(f32))
        for t in range(DEC_SEQ):
            conv = convb_ref[...] + convw_ref[0:1, :] * zp[t]
            conv = conv + convw_ref[1:2, :] * zp[t + 1]
            conv = conv + convw_ref[2:3, :] * zp[t + 2]
            cb = proj_ref[trows[t], OFF_CB:OFF_CC].astype(f32)
            mix_ref[trows[t], MIX_CONV:MIX_CM] = (cb * conv).astype(bf16)
        cout_ref[0] = zp[DEC_SEQ]
        cout_ref[1] = zp[DEC_SEQ + 1]

        vvs = []
        for t in range(DEC_SEQ):
            vv = _layernorm(_gelu(proj_ref[trows[t], OFF_CV:D_MAIN].astype(f32)), lng_ref[...], lnb_ref[...])
            vv_ref[trows[t], :] = vv
            vvs.append(vv)
        for t in range(DEC_SEQ):
            zc = cmb_ref[t:t + 1, :]
            for s in range(t + 1):
                w_ts = cmw_ref[t * DEC_SEQ + s:t * DEC_SEQ + s + 1, :]
                zc = zc + w_ts * vvs[s]
            u = _gelu(proj_ref[trows[t], OFF_CU:OFF_CV].astype(f32))
            mix_ref[trows[t], MIX_CM:] = (u * zc).astype(bf16)

    def seq_body(i, carry):
        seq = step * NB + i
        slot = pl.ds(pl.multiple_of(seq * SLOT, SLOT), SLOT)
        eg_row = eg_s[pl.ds(seq, 1), :]
        for h in range(GLA_HEADS):
            dk = slice(h * GLA_DK, (h + 1) * GLA_DK)
            dv = slice(h * GLA_DV, (h + 1) * GLA_DV)
            s0 = sin_ref[i, h]
            o_seq = _dot(qb_s[h, slot, :].astype(bf16), s0.astype(bf16))
            ob_s[2 * h, slot, :] = o_seq[:, :LANES_V7X]
            ob_s[2 * h + 1, slot, :] = o_seq[:, LANES_V7X:]
            v_seq = jnp.concatenate([vb_s[2 * h, slot, :], vb_s[2 * h + 1, slot, :]], axis=1)
            ds = _dot_tn(kb_s[h, slot, :].astype(bf16), v_seq.astype(bf16))
            eg_rows = jnp.transpose(jnp.broadcast_to(eg_row[:, dk], (GLA_DK, GLA_DK)))
            sout_ref[i, h] = jnp.concatenate([eg_rows, eg_rows], axis=1) * s0 + ds
        return carry

    lax.fori_loop(0, NB, seq_body, 0, unroll=4)

    @pl.when(step == pl.num_programs(0) - 1)
    def _():
        for t in range(DEC_SEQ):
            seq_rows = pl.ds(t, DEC_BATCH, stride=SLOT)
            o_t = oi_s[trows[t], :] + jnp.concatenate(
                [ob_s[j, seq_rows, :] for j in range(D_GLA // LANES_V7X)], axis=1)
            for h in range(GLA_HEADS):
                dv = slice(h * GLA_DV, (h + 1) * GLA_DV)
                o = o_t[:, dv]
                o = o * lax.rsqrt(jnp.mean(o * o, axis=-1, keepdims=True) + EPS) * glag_ref[:, dv]
                r_h = proj_ref[trows[t], OFF_R + h * GLA_DV:OFF_R + (h + 1) * GLA_DV].astype(f32)
                mix_ref[trows[t], dv] = (o * jax.nn.silu(r_h)).astype(bf16)


def sample_mixer(proj, pa, state, new_state, cbuf, wa2, ba, glag, convw, convb, lng, lnb, cmw, cmb, l):
    const = lambda shape: pl.BlockSpec(shape, lambda j: (0,) * len(shape))
    seq_rows = DEC_BATCH * SLOT
    state_block = pl.BlockSpec((None, NB, GLA_HEADS, GLA_DK, GLA_DV), lambda j: (l, j, 0, 0, 0))
    return pl.pallas_call(
        _sample_mixer_kernel,
        grid=(DEC_BATCH // NB,),
        in_specs=[pl.BlockSpec((N_SAMPLE, D_MAIN), lambda j: (N_PROMPT // N_SAMPLE, 0)),
                  pl.BlockSpec((N_SAMPLE, A_PAD), lambda j: (N_PROMPT // N_SAMPLE, 0)),
                  state_block,
                  pl.BlockSpec(memory_space=pl.ANY),
                  const((CONV_W - 1, DEC_BATCH, D_CONV)),
                  const((A_PAD, GLA_HEADS * GLA_DK)), const((1, GLA_HEADS * GLA_DK)),
                  const((1, D_GLA)), const((CONV_W, D_CONV)), const((1, D_CONV)),
                  const((1, D_CM)), const((1, D_CM)),
                  const((DEC_SEQ * DEC_SEQ, D_CM)), const((CM_CHUNK, D_CM))],
        out_specs=[const((N_SAMPLE, D_MODEL)),
                   state_block,
                   const((CONV_W - 1, DEC_BATCH, D_CONV)),
                   const((N_SAMPLE, D_CM))],
        out_shape=[jax.ShapeDtypeStruct((N_SAMPLE, D_MODEL), jnp.bfloat16),
                   jax.ShapeDtypeStruct((DEPTH, DEC_BATCH, GLA_HEADS, GLA_DK, GLA_DV), jnp.float32),
                   jax.ShapeDtypeStruct((CONV_W - 1, DEC_BATCH, D_CONV), jnp.float32),
                   jax.ShapeDtypeStruct((N_SAMPLE, D_CM), jnp.float32)],
        scratch_shapes=[pltpu.VMEM((GLA_HEADS, seq_rows, LANES_V7X), jnp.float32),
                        pltpu.VMEM((GLA_HEADS, seq_rows, LANES_V7X), jnp.float32),
                        pltpu.VMEM((D_GLA // LANES_V7X, seq_rows, LANES_V7X), jnp.float32),
                        pltpu.VMEM((D_GLA // LANES_V7X, seq_rows, LANES_V7X), jnp.float32),
                        pltpu.VMEM((N_SAMPLE, D_GLA), jnp.float32),
                        pltpu.VMEM((DEC_BATCH, GLA_HEADS * GLA_DK), jnp.float32)],
        input_output_aliases={3: 1},
        compiler_params=_params(("arbitrary",)),
        name="sample_mixer",
    )(proj, pa, state, new_state, cbuf, wa2, ba, glag, convw, convb, lng, lnb, cmw, cmb)


def _time_major(x):
    return jnp.swapaxes(x, 0, 1).reshape(N_SAMPLE, x.shape[-1])


def _batch_major(x):
    return jnp.swapaxes(x.reshape(DEC_SEQ, DEC_BATCH, x.shape[-1]), 0, 1)


PACK_ROWS = 256


def _pack_w_in_kernel(w_ref, main_ref, gate_ref):
    a0, a1 = OFF_CB, OFF_CB + GLA_RANK
    w = w_ref[...]
    main_ref[:, :a0] = w[:, :a0].astype(main_ref.dtype)
    main_ref[:, a0:] = w[:, a1:].astype(main_ref.dtype)
    gate = jnp.concatenate([w[:, a0:a1], jnp.zeros((PACK_ROWS, A_PAD - GLA_RANK), w.dtype)], axis=1)
    gate_ref[...] = gate.astype(gate_ref.dtype)


def _pack_w_in(w_in):
    d_in = w_in.shape[-1]
    return pl.pallas_call(
        _pack_w_in_kernel,
        grid=(DEPTH, D_MODEL // PACK_ROWS),
        in_specs=[pl.BlockSpec((None, PACK_ROWS, d_in), lambda l, i: (l, i, 0))],
        out_specs=[pl.BlockSpec((None, PACK_ROWS, D_MAIN), lambda l, i: (l, i, 0)),
                   pl.BlockSpec((None, PACK_ROWS, A_PAD), lambda l, i: (l, i, 0))],
        out_shape=[jax.ShapeDtypeStruct((DEPTH, D_MODEL, D_MAIN), jnp.bfloat16),
                   jax.ShapeDtypeStruct((DEPTH, D_MODEL, A_PAD), jnp.bfloat16)],
        compiler_params=_params(("arbitrary", "arbitrary")),
        name="pack_w_in",
    )(w_in)


def kernel(x_prompt, x_sample, state_gla, state_conv, norm1_g, w_in, w_a2, b_a, gla_g, conv_w, conv_b,
           cm_ln_g, cm_ln_b, cm_ws, cm_bs, w_out, norm2_g, w_gate, w_up, w_down, final_g):
    bf = jnp.bfloat16
    w_main, w_gate_rank = _pack_w_in(w_in)
    w_out_b, w_down_b = w_out.astype(bf), w_down.astype(bf)
    wa2_p = jnp.pad(w_a2, ((0, 0), (0, A_PAD - GLA_RANK), (0, 0))).astype(bf)
    cmb_rows = jnp.repeat(jnp.swapaxes(cm_bs, 1, 2), CM_HD, axis=2)
    cmw_small = jnp.repeat(jnp.transpose(cm_ws[:, :, :DEC_SEQ, :DEC_SEQ], (0, 2, 3, 1))
                           .reshape(DEPTH, DEC_SEQ * DEC_SEQ, CM_HEADS), CM_HD, axis=2)
    cbuf_tm = jnp.swapaxes(state_conv, 1, 2)
    row = lambda a: a.reshape(1, -1)

    xp, xs, xs_block = x_prompt.reshape(N_PROMPT, D_MODEL), _time_major(x_sample), 0
    gla_p, conv_p, conv_s, cmv_s = [], [], [], []
    gla_s = jnp.zeros(state_gla.shape, state_gla.dtype)
    for l in range(DEPTH):
        lw = (wa2_p[l], row(b_a[l]), row(gla_g[l]), conv_w[l], row(conv_b[l]), row(cm_ln_g[l]), row(cm_ln_b[l]))
        proj, pa = in_proj(xp, xs, xs_block, row(norm1_g[l]), w_main, w_gate_rank, l)
        mix_p, Sp, cp = prompt_mixer(proj, pa, *lw, cm_ws[l], cmb_rows[l])
        mix_s, gla_s, cs, vs = sample_mixer(proj, pa, state_gla, gla_s, cbuf_tm[l], *lw,
                                            cmw_small[l], cmb_rows[l], l)
        h, hn = out_proj(mix_p, mix_s, w_out_b, xp, xs, xs_block, row(norm2_g[l]), l)
        act = ffn_up(hn, w_gate, w_up, l)
        x = ffn_down(act, w_down_b, h, l)
        xp, xs, xs_block = x, x, N_PROMPT_TILES
        gla_p.append(Sp); conv_p.append(cp)
        conv_s.append(jnp.swapaxes(cs, 0, 1)); cmv_s.append(_batch_major(vs))
    y_p, y_s = final_norm(x, row(final_g))
    y_prompt = y_p.reshape(BATCH, SEQ, D_MODEL)
    y_sample = _batch_major(y_s)
    return (y_prompt, y_sample, jnp.stack(gla_p), jnp.stack(conv_p), gla_s,
            jnp.stack(conv_s), jnp.stack(cmv_s))
```

```python
import math

import jax
import jax.numpy as jnp
from jax import lax
from jax.experimental import pallas as pl
from jax.experimental.pallas import tpu as pltpu

D_MODEL = 2048
BATCH = 4
SEQ = 2048
DEPTH = 4
DEC_BATCH = 128
DEC_SEQ = 4
D_GLA = 1024
GLA_HEADS = 4
GLA_DK = 128
GLA_DV = 256
GLA_RANK = 16
GLA_TAU = 16.0
GLA_CHUNK = 64
D_CONV = 512
CONV_W = 3
D_CM = 512
CM_HEADS = 4
CM_HD = 128
CM_CHUNK = 128
D_FF = 5632
EPS = 1e-6
SQRT_HALF = math.sqrt(0.5)

N_PROMPT = BATCH * SEQ
N_SAMPLE = DEC_BATCH * DEC_SEQ
N_TOK = N_PROMPT + N_SAMPLE

OFF_Q, OFF_K, OFF_V, OFF_R = 0, 512, 1024, 2048
OFF_CB, OFF_CC, OFF_CH, OFF_CU, OFF_CV = 3072, 3584, 4096, 4608, 5120
D_MAIN = 5632
A_PAD = 128
MIX_CONV, MIX_CM = D_GLA, D_GLA + D_CONV

LANES_V7X = 128
SUBLANES_V7X = 8
VMEM_LIMIT_V7X = 56 * 1024 * 1024

TM = 1088
TM_OUT = 512
TN_IN = D_MAIN // 2
TN_FF = 512
TN_DOWN = 512
NORM_ROWS = 32
TT = 512
GLA_GROUP = 256


def _params(sem):
    return pltpu.CompilerParams(dimension_semantics=sem, vmem_limit_bytes=VMEM_LIMIT_V7X)


def _rms_rows(x_ref, g_ref, o_ref, rows):
    g = g_ref[...]

    def body(c, carry):
        sl = pl.ds(pl.multiple_of(c * NORM_ROWS, NORM_ROWS), NORM_ROWS)
        xf = x_ref[sl, :]
        ms = jnp.mean(xf * xf, axis=-1, keepdims=True)
        o_ref[sl, :] = (xf * lax.rsqrt(ms + EPS) * g).astype(o_ref.dtype)
        return carry

    lax.fori_loop(0, rows // NORM_ROWS, body, 0)


def _rms_rows_unrolled(load_rows, g, o_ref, rows):
    for c in range(rows // NORM_ROWS):
        sl = slice(c * NORM_ROWS, (c + 1) * NORM_ROWS)
        xf = load_rows(sl)
        ms = jnp.mean(xf * xf, axis=-1, keepdims=True)
        o_ref[sl, :] = (xf * lax.rsqrt(ms + EPS) * g).astype(o_ref.dtype)


def _row_tile_specs(sample_block, index):
    prompt = pl.BlockSpec((TM_OUT, D_MODEL),
                          lambda *ids: (jnp.minimum(index(*ids), N_PROMPT_TILES - 1), 0))
    sample = pl.BlockSpec((TM_OUT, D_MODEL), lambda *ids: (sample_block, 0),
                          pipeline_mode=pl.Buffered(1))
    return prompt, sample


N_PROMPT_TILES = N_PROMPT // TM_OUT
N_ROW_TILES = N_TOK // TM_OUT


def _in_proj_kernel(x0_ref, xp_ref, xs_ref, g_ref, w_ref, wa_ref, o_ref, oa_ref, xn_a, xn_b):
    j, i = pl.program_id(0), pl.program_id(1)
    g = g_ref[...]

    @pl.when((i == 0) & (j == 0))
    def _():
        _rms_rows(x0_ref, g_ref, xn_a, TM_OUT)

    next_is_sample = i == N_PROMPT_TILES - 1

    def load_next(sl):
        return jnp.where(next_is_sample, xs_ref[sl, :], xp_ref[sl, :])

    def step(cur, nxt):
        _rms_rows_unrolled(load_next, g, nxt, TM_OUT)
        o_ref[...] = jnp.dot(cur[...], w_ref[...],
                             preferred_element_type=jnp.float32).astype(o_ref.dtype)

        @pl.when(j == 0)
        def _():
            oa_ref[...] = jnp.dot(cur[...], wa_ref[...],
                                  preferred_element_type=jnp.float32).astype(oa_ref.dtype)

    even = (j * N_ROW_TILES + i) % 2 == 0
    pl.when(even)(lambda: step(xn_a, xn_b))
    pl.when(jnp.logical_not(even))(lambda: step(xn_b, xn_a))


def in_proj(xp, xs, xs_block, g, w, wa, l):
    nxt = lambda j, i: (i + 1) % N_ROW_TILES
    xp_spec, xs_spec = _row_tile_specs(xs_block, nxt)
    return pl.pallas_call(
        _in_proj_kernel,
        grid=(D_MAIN // TN_IN, N_ROW_TILES),
        in_specs=[pl.BlockSpec((TM_OUT, D_MODEL), lambda j, i: (0, 0), pipeline_mode=pl.Buffered(1)),
                  xp_spec, xs_spec,
                  pl.BlockSpec((1, D_MODEL), lambda j, i: (0, 0)),
                  pl.BlockSpec((None, D_MODEL, TN_IN), lambda j, i: (l, 0, j)),
                  pl.BlockSpec((None, D_MODEL, A_PAD), lambda j, i: (l, 0, 0))],
        out_specs=[pl.BlockSpec((TM_OUT, TN_IN), lambda j, i: (i, j)),
                   pl.BlockSpec((TM_OUT, A_PAD), lambda j, i: (jnp.where(j == 0, i, N_ROW_TILES - 1), 0))],
        out_shape=[jax.ShapeDtypeStruct((N_TOK, D_MAIN), jnp.bfloat16),
                   jax.ShapeDtypeStruct((N_TOK, A_PAD), jnp.bfloat16)],
        scratch_shapes=[pltpu.VMEM((TM_OUT, D_MODEL), jnp.bfloat16),
                        pltpu.VMEM((TM_OUT, D_MODEL), jnp.bfloat16)],
        compiler_params=_params(("arbitrary", "arbitrary")),
        name="in_proj",
    )(xp, xp, xs, g, w, wa)


def _out_proj_kernel(mixp_ref, mixs_ref, w_ref, xp_ref, xs_ref, g_ref, h_ref, hn_ref, hs_a, hs_b):
    i = pl.program_id(0)
    g = g_ref[...]
    is_sample = i == N_PROMPT_TILES

    def matmul(dst):
        mix = jnp.where(is_sample, mixs_ref[...], mixp_ref[...])
        x = jnp.where(is_sample, xs_ref[...], xp_ref[...])
        dst[...] = x + jnp.dot(mix, w_ref[...], preferred_element_type=jnp.float32)

    def finish(src):
        def load(sl):
            rows = src[sl, :]
            h_ref[sl, :] = rows
            return rows
        _rms_rows_unrolled(load, g, hn_ref, TM_OUT)

    odd = i % 2 == 1
    last = i == N_ROW_TILES

    @pl.when(i == 0)
    def _():
        matmul(hs_a)

    @pl.when(odd & jnp.logical_not(last))
    def _():
        matmul(hs_b)
        finish(hs_a)

    @pl.when(jnp.logical_not(odd) & (i > 0))
    def _():
        matmul(hs_a)
        finish(hs_b)

    @pl.when(last)
    def _():
        finish(hs_a)


def out_proj(mix_p, mix_s, w, xp, xs, xs_block, g, l):
    assert N_ROW_TILES % 2 == 1
    cur = lambda i: jnp.minimum(i, N_ROW_TILES - 1)
    xp_spec, xs_spec = _row_tile_specs(xs_block, cur)
    done = lambda i: (jnp.maximum(i - 1, 0), 0)
    return pl.pallas_call(
        _out_proj_kernel,
        grid=(N_ROW_TILES + 1,),
        in_specs=[pl.BlockSpec((TM_OUT, D_MODEL), lambda i: (jnp.minimum(i, N_PROMPT_TILES - 1), 0)),
                  pl.BlockSpec((N_SAMPLE, D_MODEL), lambda i: (0, 0), pipeline_mode=pl.Buffered(1)),
                  pl.BlockSpec((None, D_MODEL, D_MODEL), lambda i: (l, 0, 0), pipeline_mode=pl.Buffered(1)),
                  xp_spec, xs_spec,
                  pl.BlockSpec((1, D_MODEL), lambda i: (0, 0))],
        out_specs=[pl.BlockSpec((TM_OUT, D_MODEL), done),
                   pl.BlockSpec((TM_OUT, D_MODEL), done)],
        out_shape=[jax.ShapeDtypeStruct((N_TOK, D_MODEL), jnp.float32),
                   jax.ShapeDtypeStruct((N_TOK, D_MODEL), jnp.bfloat16)],
        scratch_shapes=[pltpu.VMEM((TM_OUT, D_MODEL), jnp.float32),
                        pltpu.VMEM((TM_OUT, D_MODEL), jnp.float32)],
        compiler_params=_params(("arbitrary",)),
        name="out_proj",
    )(mix_p, mix_s, w, xp, xs, g)


def _ffn_up_kernel(hn_ref, wg_ref, wu_ref, o_ref, wgb_ref, wub_ref):
    @pl.when(pl.program_id(1) == 0)
    def _():
        wgb_ref[...] = wg_ref[...].astype(jnp.bfloat16)
        wub_ref[...] = wu_ref[...].astype(jnp.bfloat16)

    hn = hn_ref[...]
    gate = jnp.dot(hn, wgb_ref[...], preferred_element_type=jnp.float32)
    up = jnp.dot(hn, wub_ref[...], preferred_element_type=jnp.float32)
    o_ref[...] = (jax.nn.silu(gate) * up).astype(o_ref.dtype)


def ffn_up(hn, wg, wu, l):
    return pl.pallas_call(
        _ffn_up_kernel,
        grid=(D_FF // TN_FF, N_TOK // TM),
        in_specs=[pl.BlockSpec((TM, D_MODEL), lambda j, i: (i, 0)),
                  pl.BlockSpec((None, D_MODEL, TN_FF), lambda j, i: (l, 0, j)),
                  pl.BlockSpec((None, D_MODEL, TN_FF), lambda j, i: (l, 0, j))],
        out_specs=pl.BlockSpec((TM, TN_FF), lambda j, i: (i, j)),
        out_shape=jax.ShapeDtypeStruct((N_TOK, D_FF), jnp.bfloat16),
        scratch_shapes=[pltpu.VMEM((D_MODEL, TN_FF), jnp.bfloat16),
                        pltpu.VMEM((D_MODEL, TN_FF), jnp.bfloat16)],
        compiler_params=_params(("arbitrary", "arbitrary")),
        name="ffn_up",
    )(hn, wg, wu)


def _ffn_down_kernel(a_ref, w_ref, h_ref, o_ref):
    o_ref[...] = h_ref[...] + jnp.dot(a_ref[...], w_ref[...],
                                      preferred_element_type=jnp.float32)


def ffn_down(act, w, h, l):
    return pl.pallas_call(
        _ffn_down_kernel,
        grid=(N_TOK // TM, D_MODEL // TN_DOWN),
        in_specs=[pl.BlockSpec((TM, D_FF), lambda i, j: (i, 0)),
                  pl.BlockSpec((None, D_FF, TN_DOWN), lambda i, j: (l, 0, j)),
                  pl.BlockSpec((TM, TN_DOWN), lambda i, j: (i, j))],
        out_specs=pl.BlockSpec((TM, TN_DOWN), lambda i, j: (i, j)),
        out_shape=jax.ShapeDtypeStruct((N_TOK, D_MODEL), jnp.float32),
        compiler_params=_params(("arbitrary", "arbitrary")),
        name="ffn_down",
    )(act, w, h)


def _final_norm_kernel(x_ref, g_ref, yp_ref, ys_ref):
    is_prompt = pl.program_id(0) < N_PROMPT_TILES
    pl.when(is_prompt)(lambda: _rms_rows(x_ref, g_ref, yp_ref, TM_OUT))
    pl.when(jnp.logical_not(is_prompt))(lambda: _rms_rows(x_ref, g_ref, ys_ref, TM_OUT))


def final_norm(x, g):
    return pl.pallas_call(
        _final_norm_kernel,
        grid=(N_TOK // TM_OUT,),
        in_specs=[pl.BlockSpec((TM_OUT, D_MODEL), lambda i: (i, 0)),
                  pl.BlockSpec((1, D_MODEL), lambda i: (0, 0))],
        out_specs=[pl.BlockSpec((TM_OUT, D_MODEL), lambda i: (jnp.minimum(i, N_PROMPT_TILES - 1), 0)),
                   pl.BlockSpec((N_SAMPLE, D_MODEL), lambda i: (0, 0))],
        out_shape=[jax.ShapeDtypeStruct((N_PROMPT, D_MODEL), jnp.float32),
                   jax.ShapeDtypeStruct((N_SAMPLE, D_MODEL), jnp.float32)],
        compiler_params=_params(("arbitrary",)),
        name="final_norm",
    )(x, g)


def _log_sigmoid(z):
    return jnp.minimum(z, 0.0) - jnp.log1p(jnp.exp(-jnp.abs(z)))


def _gelu(x):
    return 0.5 * x * (1.0 + lax.erf(x * SQRT_HALF))


def _layernorm(x, g, b):
    mu = jnp.mean(x, axis=-1, keepdims=True)
    xc = x - mu
    return xc * lax.rsqrt(jnp.mean(xc * xc, axis=-1, keepdims=True) + EPS) * g + b


def _dot(a, b):
    return jnp.dot(a, b, preferred_element_type=jnp.float32)


def _dot_nt(a, b):
    return lax.dot_general(a, b, (((1,), (1,)), ((), ())), preferred_element_type=jnp.float32)


def _dot_tn(a, b):
    return lax.dot_general(a, b, (((0,), (0,)), ((), ())), preferred_element_type=jnp.float32)


def _tril(n):
    r = lax.broadcasted_iota(jnp.int32, (n, n), 0)
    c = lax.broadcasted_iota(jnp.int32, (n, n), 1)
    return r >= c


def _prompt_mixer_kernel(proj_ref, pa_ref, wa2_ref, ba_ref, glag_ref, convw_ref, convb_ref, lng_ref, lnb_ref,
                         cmw_ref, cmb_ref, mix_ref, sout_ref, cout_ref, s_ref, zprev_ref,
                         qt_s, kd_s, b_s, oi_s):
    f32, bf16 = jnp.float32, jnp.bfloat16
    t_tile = pl.program_id(1)

    @pl.when(t_tile == 0)
    def _():
        s_ref[...] = jnp.zeros_like(s_ref)
        zprev_ref[...] = jnp.zeros_like(zprev_ref)

    q_scale = GLA_DK ** -0.5
    n_chunks = TT // GLA_CHUNK
    hk = GLA_HEADS * GLA_DK
    head_dk = [slice(h * GLA_DK, (h + 1) * GLA_DK) for h in range(GLA_HEADS)]
    head_dv = [slice(h * GLA_DV, (h + 1) * GLA_DV) for h in range(GLA_HEADS)]

    r = lax.broadcasted_iota(jnp.int32, (GLA_GROUP, GLA_GROUP), 0)
    c = lax.broadcasted_iota(jnp.int32, (GLA_GROUP, GLA_GROUP), 1)
    chunk_shift = GLA_CHUNK.bit_length() - 1
    causal = ((r >> chunk_shift) == (c >> chunk_shift)) & (r >= c)
    causal_ones = causal.astype(bf16)
    group_chunks = GLA_GROUP // GLA_CHUNK

    for grp in range(TT // GLA_GROUP):
        rows = slice(grp * GLA_GROUP, (grp + 1) * GLA_GROUP)
        z = _dot(pa_ref[rows, :], wa2_ref[...]) + ba_ref[...]
        la = _log_sigmoid(z) * (1.0 / GLA_TAU)
        hi = la.astype(bf16)
        rest = la - hi.astype(f32)
        mid = rest.astype(bf16)
        lo = (rest - mid.astype(f32)).astype(bf16)
        cum = _dot(causal_ones, jnp.concatenate([hi, mid, lo], axis=1))
        b = (cum[:, :hk] + cum[:, hk:2 * hk]) + cum[:, 2 * hk:]
        b_s[rows, :] = b
        g = jnp.broadcast_to(b.reshape(group_chunks, GLA_CHUNK, hk)[:, GLA_CHUNK - 1:, :],
                             (group_chunks, GLA_CHUNK, hk)).reshape(GLA_GROUP, hk)
        q = proj_ref[rows, OFF_Q:OFF_K].astype(f32) * q_scale
        k = proj_ref[rows, OFF_K:OFF_V].astype(f32)
        qt = (q * jnp.exp(b)).astype(bf16)
        kt = (k * jnp.exp(-b)).astype(bf16)
        qt_s[rows, :] = qt
        kd_s[rows, :] = (k * jnp.exp(g - b)).astype(bf16)
        for h in range(GLA_HEADS):
            a = jnp.where(causal, _dot_nt(qt[:, head_dk[h]], kt[:, head_dk[h]]), 0.0).astype(bf16)
            oi_s[rows, head_dv[h]] = _dot(a, proj_ref[rows, OFF_V + h * GLA_DV:OFF_V + (h + 1) * GLA_DV])

    def chunk_state(ci, carry):
        rows = pl.ds(pl.multiple_of(ci * GLA_CHUNK, GLA_CHUNK), GLA_CHUNK)
        eg = jnp.exp(b_s[pl.ds(ci * GLA_CHUNK + (GLA_CHUNK - 1), 1), :])
        for h in range(GLA_HEADS):
            dk, dv = head_dk[h], head_dv[h]
            v_h = proj_ref[rows, OFF_V + h * GLA_DV:OFF_V + (h + 1) * GLA_DV]
            s_h = s_ref[h]
            o = oi_s[rows, dv] + _dot(qt_s[rows, dk], s_h.astype(bf16))
            ds = _dot_tn(kd_s[rows, dk], v_h)
            eg_rows = jnp.transpose(jnp.broadcast_to(eg[:, dk], (GLA_DK, GLA_DK)))
            s_ref[h] = jnp.concatenate([eg_rows, eg_rows], axis=1) * s_h + ds
            o = o * lax.rsqrt(jnp.mean(o * o, axis=-1, keepdims=True) + EPS) * glag_ref[:, dv]
            r_h = proj_ref[rows, OFF_R + h * GLA_DV:OFF_R + (h + 1) * GLA_DV].astype(f32)
            mix_ref[rows, dv] = (o * jax.nn.silu(r_h)).astype(bf16)
        return carry

    lax.fori_loop(0, n_chunks, chunk_state, 0, unroll=4)

    w_tril = _tril(CM_CHUNK)
    wm = [jnp.where(w_tril, cmw_ref[h], 0.0).astype(bf16) for h in range(CM_HEADS)]
    row_id = lax.broadcasted_iota(jnp.int32, (CM_CHUNK, D_CONV), 0)

    def cm_block(i, carry):
        rows = pl.ds(pl.multiple_of(i * CM_CHUNK, CM_CHUNK), CM_CHUNK)
        z = proj_ref[rows, OFF_CC:OFF_CH].astype(f32) * proj_ref[rows, OFF_CH:OFF_CU].astype(f32)
        prev = zprev_ref[...]
        p1 = prev[SUBLANES_V7X - 1:SUBLANES_V7X, :]
        p2 = prev[SUBLANES_V7X - 2:SUBLANES_V7X - 1, :]
        z1 = jnp.where(row_id == 0, p1, pltpu.roll(z, 1, 0))
        z2 = jnp.where(row_id == 0, p2, jnp.where(row_id == 1, p1, pltpu.roll(z, 2, 0)))
        conv = convb_ref[...] + convw_ref[0:1, :] * z2
        conv = conv + convw_ref[1:2, :] * z1
        conv = conv + convw_ref[2:3, :] * z
        cb = proj_ref[rows, OFF_CB:OFF_CC].astype(f32)
        mix_ref[rows, MIX_CONV:MIX_CM] = (cb * conv).astype(bf16)
        zprev_ref[...] = z[CM_CHUNK - SUBLANES_V7X:, :]
        cout_ref[0] = z[CM_CHUNK - (CONV_W - 1):, :]

        u = _gelu(proj_ref[rows, OFF_CU:OFF_CV].astype(f32))
        vv = _layernorm(_gelu(proj_ref[rows, OFF_CV:D_MAIN].astype(f32)), lng_ref[...], lnb_ref[...])
        vvb = vv.astype(bf16)
        for h in range(CM_HEADS):
            hd = slice(h * CM_HD, (h + 1) * CM_HD)
            zc = _dot(wm[h], vvb[:, hd]) + cmb_ref[:, hd]
            mix_ref[rows, MIX_CM + h * CM_HD:MIX_CM + (h + 1) * CM_HD] = (u[:, hd] * zc).astype(bf16)
        return carry

    lax.fori_loop(0, TT // CM_CHUNK, cm_block, 0, unroll=4)

    @pl.when(t_tile == pl.num_programs(1) - 1)
    def _():
        sout_ref[0] = s_ref[...]


def prompt_mixer(proj, pa, wa2, ba, glag, convw, convb, lng, lnb, cmw, cmb):
    const = lambda shape: pl.BlockSpec(shape, lambda b, t: (0,) * len(shape))
    tiles = SEQ // TT
    return pl.pallas_call(
        _prompt_mixer_kernel,
        grid=(BATCH, tiles),
        in_specs=[pl.BlockSpec((TT, D_MAIN), lambda b, t: (b * tiles + t, 0)),
                  pl.BlockSpec((TT, A_PAD), lambda b, t: (b * tiles + t, 0)),
                  const((A_PAD, GLA_HEADS * GLA_DK)), const((1, GLA_HEADS * GLA_DK)),
                  const((1, D_GLA)), const((CONV_W, D_CONV)), const((1, D_CONV)),
                  const((1, D_CM)), const((1, D_CM)),
                  const((CM_HEADS, CM_CHUNK, CM_CHUNK)), const((CM_CHUNK, D_CM))],
        out_specs=[pl.BlockSpec((TT, D_MODEL), lambda b, t: (b * tiles + t, 0)),
                   pl.BlockSpec((1, GLA_HEADS, GLA_DK, GLA_DV), lambda b, t: (b, 0, 0, 0)),
                   pl.BlockSpec((1, CONV_W - 1, D_CONV), lambda b, t: (b, 0, 0))],
        out_shape=[jax.ShapeDtypeStruct((N_PROMPT, D_MODEL), jnp.bfloat16),
                   jax.ShapeDtypeStruct((BATCH, GLA_HEADS, GLA_DK, GLA_DV), jnp.float32),
                   jax.ShapeDtypeStruct((BATCH, CONV_W - 1, D_CONV), jnp.float32)],
        scratch_shapes=[pltpu.VMEM((GLA_HEADS, GLA_DK, GLA_DV), jnp.float32),
                        pltpu.VMEM((SUBLANES_V7X, D_CONV), jnp.float32),
                        pltpu.VMEM((TT, GLA_HEADS * GLA_DK), jnp.bfloat16),
                        pltpu.VMEM((TT, GLA_HEADS * GLA_DK), jnp.bfloat16),
                        pltpu.VMEM((TT, GLA_HEADS * GLA_DK), jnp.float32),
                        pltpu.VMEM((TT, D_GLA), jnp.float32)],
        compiler_params=_params(("arbitrary", "arbitrary")),
        name="prompt_mixer",
    )(proj, pa, wa2, ba, glag, convw, convb, lng, lnb, cmw, cmb)


NB = 8
SLOT = SUBLANES_V7X


def _sample_mixer_kernel(proj_ref, pa_ref, sin_ref, _new_state_hbm, cbuf_ref, wa2_ref, ba_ref, glag_ref, convw_ref, convb_ref,
                         lng_ref, lnb_ref, cmw_ref, cmb_ref,
                         mix_ref, sout_ref, cout_ref, vv_ref,
                         qb_s, kb_s, vb_s, ob_s, oi_s, eg_s):
    f32, bf16 = jnp.float32, jnp.bfloat16
    step = pl.program_id(0)
    q_scale = GLA_DK ** -0.5
    trows = [slice(t * DEC_BATCH, (t + 1) * DEC_BATCH) for t in range(DEC_SEQ)]

    @pl.when(step == 0)
    def _():
        qb_s[...] = jnp.zeros_like(qb_s)
        kb_s[...] = jnp.zeros_like(kb_s)
        vb_s[...] = jnp.zeros_like(vb_s)

        cum, b = [], None
        for t in range(DEC_SEQ):
            z = _dot(pa_ref[trows[t], :], wa2_ref[...]) + ba_ref[...]
            la = _log_sigmoid(z) * (1.0 / GLA_TAU)
            b = la if b is None else b + la
            cum.append(b)
        g = cum[-1]
        eg_s[...] = jnp.exp(g)
        qt, kt, vs = [], [], []
        for t in range(DEC_SEQ):
            q = proj_ref[trows[t], OFF_Q:OFF_K].astype(f32) * q_scale
            k = proj_ref[trows[t], OFF_K:OFF_V].astype(f32)
            v = proj_ref[trows[t], OFF_V:OFF_R].astype(f32)
            qt.append(q * jnp.exp(cum[t]))
            kt.append(k * jnp.exp(-cum[t]))
            vs.append(v)
            seq_rows = pl.ds(t, DEC_BATCH, stride=SLOT)
            kd = k * jnp.exp(g - cum[t])
            for j in range(GLA_HEADS):
                qb_s[j, seq_rows, :] = qt[t][:, j * LANES_V7X:(j + 1) * LANES_V7X]
                kb_s[j, seq_rows, :] = kd[:, j * LANES_V7X:(j + 1) * LANES_V7X]
            for j in range(D_GLA // LANES_V7X):
                vb_s[j, seq_rows, :] = v[:, j * LANES_V7X:(j + 1) * LANES_V7X]
        for t in range(DEC_SEQ):
            heads = []
            for h in range(GLA_HEADS):
                dk = slice(h * GLA_DK, (h + 1) * GLA_DK)
                dv = slice(h * GLA_DV, (h + 1) * GLA_DV)
                acc = None
                for s in range(t + 1):
                    a_ts = jnp.sum(qt[t][:, dk] * kt[s][:, dk], axis=-1, keepdims=True)
                    term = a_ts * vs[s][:, dv]
                    acc = term if acc is None else acc + term
                heads.append(acc)
            oi_s[trows[t], :] = jnp.concatenate(heads, axis=1)

        zp = [cbuf_ref[0], cbuf_ref[1]]
        for t in range(DEC_SEQ):
            zp.append(proj_ref[trows[t], OFF_CC:OFF_CH].astype(f32)
                      * proj_ref[trows[t], OFF_CH:OFF_CU].astype(f32))
        for t in range(DEC_SEQ):
            conv = convb_ref[...] + convw_ref[0:1, :] * zp[t]
            conv = conv + convw_ref[1:2, :] * zp[t + 1]
            conv = conv + convw_ref[2:3, :] * zp[t + 2]
            cb = proj_ref[trows[t], OFF_CB:OFF_CC].astype(f32)
            mix_ref[trows[t], MIX_CONV:MIX_CM] = (cb * conv).astype(bf16)
        cout_ref[0] = zp[DEC_SEQ]
        cout_ref[1] = zp[DEC_SEQ + 1]

        vvs = []
        for t in range(DEC_SEQ):
            vv = _layernorm(_gelu(proj_ref[trows[t], OFF_CV:D_MAIN].astype(f32)), lng_ref[...], lnb_ref[...])
            vv_ref[trows[t], :] = vv
            vvs.append(vv)
        for t in range(DEC_SEQ):
            zc = cmb_ref[t:t + 1, :]
            for s in range(t + 1):
                w_ts = cmw_ref[t * DEC_SEQ + s:t * DEC_SEQ + s + 1, :]
                zc = zc + w_ts * vvs[s]
            u = _gelu(proj_ref[trows[t], OFF_CU:OFF_CV].astype(f32))
            mix_ref[trows[t], MIX_CM:] = (u * zc).astype(bf16)

    def seq_body(i, carry):
        seq = step * NB + i
        slot = pl.ds(pl.multiple_of(seq * SLOT, SLOT), SLOT)
        eg_row = eg_s[pl.ds(seq, 1), :]
        for h in range(GLA_HEADS):
            dk = slice(h * GLA_DK, (h + 1) * GLA_DK)
            dv = slice(h * GLA_DV, (h + 1) * GLA_DV)
            s0 = sin_ref[i, h]
            o_seq = _dot(qb_s[h, slot, :].astype(bf16), s0.astype(bf16))
            ob_s[2 * h, slot, :] = o_seq[:, :LANES_V7X]
            ob_s[2 * h + 1, slot, :] = o_seq[:, LANES_V7X:]
            v_seq = jnp.concatenate([vb_s[2 * h, slot, :], vb_s[2 * h + 1, slot, :]], axis=1)
            ds = _dot_tn(kb_s[h, slot, :].astype(bf16), v_seq.astype(bf16))
            eg_rows = jnp.transpose(jnp.broadcast_to(eg_row[:, dk], (GLA_DK, GLA_DK)))
            sout_ref[i, h] = jnp.concatenate([eg_rows, eg_rows], axis=1) * s0 + ds
        return carry

    lax.fori_loop(0, NB, seq_body, 0, unroll=4)

    @pl.when(step == pl.num_programs(0) - 1)
    def _():
        for t in range(DEC_SEQ):
            seq_rows = pl.ds(t, DEC_BATCH, stride=SLOT)
            o_t = oi_s[trows[t], :] + jnp.concatenate(
                [ob_s[j, seq_rows, :] for j in range(D_GLA // LANES_V7X)], axis=1)
            for h in range(GLA_HEADS):
                dv = slice(h * GLA_DV, (h + 1) * GLA_DV)
                o = o_t[:, dv]
                o = o * lax.rsqrt(jnp.mean(o * o, axis=-1, keepdims=True) + EPS) * glag_ref[:, dv]
                r_h = proj_ref[trows[t], OFF_R + h * GLA_DV:OFF_R + (h + 1) * GLA_DV].astype(f32)
                mix_ref[trows[t], dv] = (o * jax.nn.silu(r_h)).astype(bf16)


def sample_mixer(proj, pa, state, new_state, cbuf, wa2, ba, glag, convw, convb, lng, lnb, cmw, cmb, l):
    const = lambda shape: pl.BlockSpec(shape, lambda j: (0,) * len(shape))
    seq_rows = DEC_BATCH * SLOT
    state_block = pl.BlockSpec((None, NB, GLA_HEADS, GLA_DK, GLA_DV), lambda j: (l, j, 0, 0, 0))
    return pl.pallas_call(
        _sample_mixer_kernel,
        grid=(DEC_BATCH // NB,),
        in_specs=[pl.BlockSpec((N_SAMPLE, D_MAIN), lambda j: (N_PROMPT // N_SAMPLE, 0)),
                  pl.BlockSpec((N_SAMPLE, A_PAD), lambda j: (N_PROMPT // N_SAMPLE, 0)),
                  state_block,
                  pl.BlockSpec(memory_space=pl.ANY),
                  const((CONV_W - 1, DEC_BATCH, D_CONV)),
                  const((A_PAD, GLA_HEADS * GLA_DK)), const((1, GLA_HEADS * GLA_DK)),
                  const((1, D_GLA)), const((CONV_W, D_CONV)), const((1, D_CONV)),
                  const((1, D_CM)), const((1, D_CM)),
                  const((DEC_SEQ * DEC_SEQ, D_CM)), const((CM_CHUNK, D_CM))],
        out_specs=[const((N_SAMPLE, D_MODEL)),
                   state_block,
                   const((CONV_W - 1, DEC_BATCH, D_CONV)),
                   const((N_SAMPLE, D_CM))],
        out_shape=[jax.ShapeDtypeStruct((N_SAMPLE, D_MODEL), jnp.bfloat16),
                   jax.ShapeDtypeStruct((DEPTH, DEC_BATCH, GLA_HEADS, GLA_DK, GLA_DV), jnp.float32),
                   jax.ShapeDtypeStruct((CONV_W - 1, DEC_BATCH, D_CONV), jnp.float32),
                   jax.ShapeDtypeStruct((N_SAMPLE, D_CM), jnp.float32)],
        scratch_shapes=[pltpu.VMEM((GLA_HEADS, seq_rows, LANES_V7X), jnp.float32),
                        pltpu.VMEM((GLA_HEADS, seq_rows, LANES_V7X), jnp.float32),
                        pltpu.VMEM((D_GLA // LANES_V7X, seq_rows, LANES_V7X), jnp.float32),
                        pltpu.VMEM((D_GLA // LANES_V7X, seq_rows, LANES_V7X), jnp.float32),
                        pltpu.VMEM((N_SAMPLE, D_GLA), jnp.float32),
                        pltpu.VMEM((DEC_BATCH, GLA_HEADS * GLA_DK), jnp.float32)],
        input_output_aliases={3: 1},
        compiler_params=_params(("arbitrary",)),
        name="sample_mixer",
    )(proj, pa, state, new_state, cbuf, wa2, ba, glag, convw, convb, lng, lnb, cmw, cmb)


def _time_major(x):
    return jnp.swapaxes(x, 0, 1).reshape(N_SAMPLE, x.shape[-1])


def _batch_major(x):
    return jnp.swapaxes(x.reshape(DEC_SEQ, DEC_BATCH, x.shape[-1]), 0, 1)


def _pack_w_in(w_in):
    a0 = OFF_CB
    a1 = a0 + GLA_RANK
    main = jnp.concatenate([w_in[..., :a0], w_in[..., a1:]], axis=-1).astype(jnp.bfloat16)
    gate = jnp.pad(w_in[..., a0:a1], ((0, 0), (0, 0), (0, A_PAD - GLA_RANK))).astype(jnp.bfloat16)
    return main, gate


def kernel(x_prompt, x_sample, state_gla, state_conv, norm1_g, w_in, w_a2, b_a, gla_g, conv_w, conv_b,
           cm_ln_g, cm_ln_b, cm_ws, cm_bs, w_out, norm2_g, w_gate, w_up, w_down, final_g):
    bf = jnp.bfloat16
    w_main, w_gate_rank = _pack_w_in(w_in)
    w_out_b, w_down_b = w_out.astype(bf), w_down.astype(bf)
    wa2_p = jnp.pad(w_a2, ((0, 0), (0, A_PAD - GLA_RANK), (0, 0))).astype(bf)
    cmb_rows = jnp.repeat(jnp.swapaxes(cm_bs, 1, 2), CM_HD, axis=2)
    cmw_small = jnp.repeat(jnp.transpose(cm_ws[:, :, :DEC_SEQ, :DEC_SEQ], (0, 2, 3, 1))
                           .reshape(DEPTH, DEC_SEQ * DEC_SEQ, CM_HEADS), CM_HD, axis=2)
    cbuf_tm = jnp.swapaxes(state_conv, 1, 2)
    row = lambda a: a.reshape(1, -1)

    xp, xs, xs_block = x_prompt.reshape(N_PROMPT, D_MODEL), _time_major(x_sample), 0
    gla_p, conv_p, conv_s, cmv_s = [], [], [], []
    gla_s = jnp.zeros(state_gla.shape, state_gla.dtype)
    for l in range(DEPTH):
        lw = (wa2_p[l], row(b_a[l]), row(gla_g[l]), conv_w[l], row(conv_b[l]), row(cm_ln_g[l]), row(cm_ln_b[l]))
        proj, pa = in_proj(xp, xs, xs_block, row(norm1_g[l]), w_main, w_gate_rank, l)
        mix_p, Sp, cp = prompt_mixer(proj, pa, *lw, cm_ws[l], cmb_rows[l])
        mix_s, gla_s, cs, vs = sample_mixer(proj, pa, state_gla, gla_s, cbuf_tm[l], *lw,
                                            cmw_small[l], cmb_rows[l], l)
        h, hn = out_proj(mix_p, mix_s, w_out_b, xp, xs, xs_block, row(norm2_g[l]), l)
        act = ffn_up(hn, w_gate, w_up, l)
        x = ffn_down(act, w_down_b, h, l)
        xp, xs, xs_block = x, x, N_PROMPT_TILES
        gla_p.append(Sp); conv_p.append(cp)
        conv_s.append(jnp.swapaxes(cs, 0, 1)); cmv_s.append(_batch_major(vs))
    y_p, y_s = final_norm(x, row(final_g))
    y_prompt = y_p.reshape(BATCH, SEQ, D_MODEL)
    y_sample = _batch_major(y_s)
    return (y_prompt, y_sample, jnp.stack(gla_p), jnp.stack(conv_p), gla_s,
            jnp.stack(conv_s), jnp.stack(cmv_s))
```

```python
import math

import jax
import jax.numpy as jnp
from jax import lax
from jax.experimental import pallas as pl
from jax.experimental.pallas import tpu as pltpu

D_MODEL = 2048
BATCH = 4
SEQ = 2048
DEPTH = 4
DEC_BATCH = 128
DEC_SEQ = 4
D_GLA = 1024
GLA_HEADS = 4
GLA_DK = 128
GLA_DV = 256
GLA_RANK = 16
GLA_TAU = 16.0
GLA_CHUNK = 64
D_CONV = 512
CONV_W = 3
D_CM = 512
CM_HEADS = 4
CM_HD = 128
CM_CHUNK = 128
D_FF = 5632
EPS = 1e-6
SQRT_HALF = math.sqrt(0.5)

N_PROMPT = BATCH * SEQ
N_SAMPLE = DEC_BATCH * DEC_SEQ
N_TOK = N_PROMPT + N_SAMPLE

OFF_Q, OFF_K, OFF_V, OFF_R = 0, 512, 1024, 2048
OFF_CB, OFF_CC, OFF_CH, OFF_CU, OFF_CV = 3072, 3584, 4096, 4608, 5120
D_MAIN = 5632
A_PAD = 128
MIX_CONV, MIX_CM = D_GLA, D_GLA + D_CONV

LANES_V7X = 128
SUBLANES_V7X = 8
VMEM_LIMIT_V7X = 56 * 1024 * 1024

TM = 1088
TM_UP = 1088
TM_OUT = 512
TN_IN = D_MAIN // 2
TN_FF = 512
TN_DOWN = 512
NORM_ROWS = 32
TT = 512
GLA_GROUP = 256


def _params(sem):
    return pltpu.CompilerParams(dimension_semantics=sem, vmem_limit_bytes=VMEM_LIMIT_V7X)


def _rms_rows(x_ref, g_ref, o_ref, rows):
    g = g_ref[...]

    def body(c, carry):
        sl = pl.ds(pl.multiple_of(c * NORM_ROWS, NORM_ROWS), NORM_ROWS)
        xf = x_ref[sl, :]
        ms = jnp.mean(xf * xf, axis=-1, keepdims=True)
        o_ref[sl, :] = (xf * lax.rsqrt(ms + EPS) * g).astype(o_ref.dtype)
        return carry

    lax.fori_loop(0, rows // NORM_ROWS, body, 0)


def _rms_rows_unrolled(load_rows, g, o_ref, rows):
    for c in range(rows // NORM_ROWS):
        sl = slice(c * NORM_ROWS, (c + 1) * NORM_ROWS)
        xf = load_rows(sl)
        ms = jnp.mean(xf * xf, axis=-1, keepdims=True)
        o_ref[sl, :] = (xf * lax.rsqrt(ms + EPS) * g).astype(o_ref.dtype)


def _row_tile_specs(sample_block, index):
    prompt = pl.BlockSpec((TM_OUT, D_MODEL),
                          lambda *ids: (jnp.minimum(index(*ids), N_PROMPT_TILES - 1), 0))
    sample = pl.BlockSpec((TM_OUT, D_MODEL), lambda *ids: (sample_block, 0),
                          pipeline_mode=pl.Buffered(1))
    return prompt, sample


N_PROMPT_TILES = N_PROMPT // TM_OUT
N_ROW_TILES = N_TOK // TM_OUT


def _in_proj_kernel(x0_ref, xp_ref, xs_ref, g_ref, w_ref, wa_ref, o_ref, oa_ref, xn_a, xn_b):
    j, i = pl.program_id(0), pl.program_id(1)
    g = g_ref[...]

    @pl.when((i == 0) & (j == 0))
    def _():
        _rms_rows(x0_ref, g_ref, xn_a, TM_OUT)

    next_is_sample = i == N_PROMPT_TILES - 1

    def load_next(sl):
        return jnp.where(next_is_sample, xs_ref[sl, :], xp_ref[sl, :])

    def step(cur, nxt):
        _rms_rows_unrolled(load_next, g, nxt, TM_OUT)
        o_ref[...] = jnp.dot(cur[...], w_ref[...],
                             preferred_element_type=jnp.float32).astype(o_ref.dtype)

        @pl.when(j == 0)
        def _():
            oa_ref[...] = jnp.dot(cur[...], wa_ref[...],
                                  preferred_element_type=jnp.float32).astype(oa_ref.dtype)

    even = (j * N_ROW_TILES + i) % 2 == 0
    pl.when(even)(lambda: step(xn_a, xn_b))
    pl.when(jnp.logical_not(even))(lambda: step(xn_b, xn_a))


def in_proj(xp, xs, xs_block, g, w, wa, l):
    nxt = lambda j, i: (i + 1) % N_ROW_TILES
    xp_spec, xs_spec = _row_tile_specs(xs_block, nxt)
    return pl.pallas_call(
        _in_proj_kernel,
        grid=(D_MAIN // TN_IN, N_ROW_TILES),
        in_specs=[pl.BlockSpec((TM_OUT, D_MODEL), lambda j, i: (0, 0), pipeline_mode=pl.Buffered(1)),
                  xp_spec, xs_spec,
                  pl.BlockSpec((1, D_MODEL), lambda j, i: (0, 0)),
                  pl.BlockSpec((None, D_MODEL, TN_IN), lambda j, i: (l, 0, j)),
                  pl.BlockSpec((None, D_MODEL, A_PAD), lambda j, i: (l, 0, 0))],
        out_specs=[pl.BlockSpec((TM_OUT, TN_IN), lambda j, i: (i, j)),
                   pl.BlockSpec((TM_OUT, A_PAD), lambda j, i: (jnp.where(j == 0, i, N_ROW_TILES - 1), 0))],
        out_shape=[jax.ShapeDtypeStruct((N_TOK, D_MAIN), jnp.bfloat16),
                   jax.ShapeDtypeStruct((N_TOK, A_PAD), jnp.bfloat16)],
        scratch_shapes=[pltpu.VMEM((TM_OUT, D_MODEL), jnp.bfloat16),
                        pltpu.VMEM((TM_OUT, D_MODEL), jnp.bfloat16)],
        compiler_params=_params(("arbitrary", "arbitrary")),
        name="in_proj",
    )(xp, xp, xs, g, w, wa)


def _out_proj_kernel(mixp_ref, mixs_ref, w_ref, xp_ref, xs_ref, g_ref, h_ref, hn_ref, hs_a, hs_b):
    i = pl.program_id(0)
    g = g_ref[...]
    is_sample = i == N_PROMPT_TILES

    def matmul(dst):
        mix = jnp.where(is_sample, mixs_ref[...], mixp_ref[...])
        x = jnp.where(is_sample, xs_ref[...], xp_ref[...])
        dst[...] = x + jnp.dot(mix, w_ref[...], preferred_element_type=jnp.float32)

    def finish(src):
        def load(sl):
            rows = src[sl, :]
            h_ref[sl, :] = rows
            return rows
        _rms_rows_unrolled(load, g, hn_ref, TM_OUT)

    odd = i % 2 == 1
    last = i == N_ROW_TILES

    @pl.when(i == 0)
    def _():
        matmul(hs_a)

    @pl.when(odd & jnp.logical_not(last))
    def _():
        matmul(hs_b)
        finish(hs_a)

    @pl.when(jnp.logical_not(odd) & (i > 0))
    def _():
        matmul(hs_a)
        finish(hs_b)

    @pl.when(last)
    def _():
        finish(hs_a)


def out_proj(mix_p, mix_s, w, xp, xs, xs_block, g, l):
    assert N_ROW_TILES % 2 == 1
    cur = lambda i: jnp.minimum(i, N_ROW_TILES - 1)
    xp_spec, xs_spec = _row_tile_specs(xs_block, cur)
    done = lambda i: (jnp.maximum(i - 1, 0), 0)
    return pl.pallas_call(
        _out_proj_kernel,
        grid=(N_ROW_TILES + 1,),
        in_specs=[pl.BlockSpec((TM_OUT, D_MODEL), lambda i: (jnp.minimum(i, N_PROMPT_TILES - 1), 0)),
                  pl.BlockSpec((N_SAMPLE, D_MODEL), lambda i: (0, 0), pipeline_mode=pl.Buffered(1)),
                  pl.BlockSpec((None, D_MODEL, D_MODEL), lambda i: (l, 0, 0), pipeline_mode=pl.Buffered(1)),
                  xp_spec, xs_spec,
                  pl.BlockSpec((1, D_MODEL), lambda i: (0, 0))],
        out_specs=[pl.BlockSpec((TM_OUT, D_MODEL), done),
                   pl.BlockSpec((TM_OUT, D_MODEL), done)],
        out_shape=[jax.ShapeDtypeStruct((N_TOK, D_MODEL), jnp.float32),
                   jax.ShapeDtypeStruct((N_TOK, D_MODEL), jnp.bfloat16)],
        scratch_shapes=[pltpu.VMEM((TM_OUT, D_MODEL), jnp.float32),
                        pltpu.VMEM((TM_OUT, D_MODEL), jnp.float32)],
        compiler_params=_params(("arbitrary",)),
        name="out_proj",
    )(mix_p, mix_s, w, xp, xs, g)


def _ffn_up_kernel(hn_ref, wg_ref, wu_ref, o_ref, wgb_ref, wub_ref):
    @pl.when(pl.program_id(1) == 0)
    def _():
        wgb_ref[...] = wg_ref[...].astype(jnp.bfloat16)
        wub_ref[...] = wu_ref[...].astype(jnp.bfloat16)

    hn = hn_ref[...]
    gate = jnp.dot(hn, wgb_ref[...], preferred_element_type=jnp.float32)
    up = jnp.dot(hn, wub_ref[...], preferred_element_type=jnp.float32)
    o_ref[...] = (jax.nn.silu(gate) * up).astype(o_ref.dtype)


def ffn_up(hn, wg, wu, l):
    return pl.pallas_call(
        _ffn_up_kernel,
        grid=(D_FF // TN_FF, N_TOK // TM_UP),
        in_specs=[pl.BlockSpec((TM_UP, D_MODEL), lambda j, i: (i, 0)),
                  pl.BlockSpec((None, D_MODEL, TN_FF), lambda j, i: (l, 0, j)),
                  pl.BlockSpec((None, D_MODEL, TN_FF), lambda j, i: (l, 0, j))],
        out_specs=pl.BlockSpec((TM_UP, TN_FF), lambda j, i: (i, j)),
        out_shape=jax.ShapeDtypeStruct((N_TOK, D_FF), jnp.bfloat16),
        scratch_shapes=[pltpu.VMEM((D_MODEL, TN_FF), jnp.bfloat16),
                        pltpu.VMEM((D_MODEL, TN_FF), jnp.bfloat16)],
        compiler_params=_params(("arbitrary", "arbitrary")),
        name="ffn_up",
    )(hn, wg, wu)


def _ffn_down_kernel(a_hbm, w_ref, h_ref, o_ref, a_buf, a_sem):
    i, j = pl.program_id(0), pl.program_id(1)
    slot = i % 2

    def act_copy(tile, dst_slot):
        rows = pl.ds(pl.multiple_of(tile * TM, TM), TM)
        return pltpu.make_async_copy(a_hbm.at[rows, :], a_buf.at[dst_slot], a_sem.at[dst_slot])

    @pl.when(j == 0)
    def _():
        @pl.when(i == 0)
        def _():
            act_copy(0, 0).start()

        act_copy(i, slot).wait()

        @pl.when(i + 1 < pl.num_programs(0))
        def _():
            act_copy(i + 1, 1 - slot).start()

    o_ref[...] = h_ref[...] + jnp.dot(a_buf[slot], w_ref[...],
                                      preferred_element_type=jnp.float32)


def ffn_down(act, w, h, l):
    return pl.pallas_call(
        _ffn_down_kernel,
        grid=(N_TOK // TM, D_MODEL // TN_DOWN),
        in_specs=[pl.BlockSpec(memory_space=pl.ANY),
                  pl.BlockSpec((None, D_FF, TN_DOWN), lambda i, j: (l, 0, j)),
                  pl.BlockSpec((TM, TN_DOWN), lambda i, j: (i, j))],
        out_specs=pl.BlockSpec((TM, TN_DOWN), lambda i, j: (i, j)),
        out_shape=jax.ShapeDtypeStruct((N_TOK, D_MODEL), jnp.float32),
        scratch_shapes=[pltpu.VMEM((2, TM, D_FF), jnp.bfloat16),
                        pltpu.SemaphoreType.DMA((2,))],
        compiler_params=_params(("arbitrary", "arbitrary")),
        name="ffn_down",
    )(act, w, h)


def _ffn_down_final_kernel(a_ref, w_ref, h_ref, g_ref, yp_ref, ys_ref):
    i, j = pl.program_id(0), pl.program_id(1)

    def run(y_ref):
        cols = pl.ds(pl.multiple_of(j * TN_DOWN, TN_DOWN), TN_DOWN)
        y_ref[:, cols] = h_ref[...] + jnp.dot(a_ref[...], w_ref[...],
                                              preferred_element_type=jnp.float32)

        @pl.when(j == pl.num_programs(1) - 1)
        def _():
            _rms_rows(y_ref, g_ref, y_ref, TM_OUT)

    is_prompt = i < N_PROMPT_TILES
    pl.when(is_prompt)(lambda: run(yp_ref))
    pl.when(jnp.logical_not(is_prompt))(lambda: run(ys_ref))


def ffn_down_final(act, w, h, g, l):
    return pl.pallas_call(
        _ffn_down_final_kernel,
        grid=(N_ROW_TILES, D_MODEL // TN_DOWN),
        in_specs=[pl.BlockSpec((TM_OUT, D_FF), lambda i, j: (i, 0)),
                  pl.BlockSpec((None, D_FF, TN_DOWN), lambda i, j: (l, 0, j)),
                  pl.BlockSpec((TM_OUT, TN_DOWN), lambda i, j: (i, j)),
                  pl.BlockSpec((1, D_MODEL), lambda i, j: (0, 0))],
        out_specs=[pl.BlockSpec((TM_OUT, D_MODEL), lambda i, j: (jnp.minimum(i, N_PROMPT_TILES - 1), 0)),
                   pl.BlockSpec((N_SAMPLE, D_MODEL), lambda i, j: (0, 0))],
        out_shape=[jax.ShapeDtypeStruct((N_PROMPT, D_MODEL), jnp.float32),
                   jax.ShapeDtypeStruct((N_SAMPLE, D_MODEL), jnp.float32)],
        compiler_params=_params(("arbitrary", "arbitrary")),
        name="ffn_down_final",
    )(act, w, h, g)


def _log_sigmoid(z):
    return jnp.minimum(z, 0.0) - jnp.log1p(jnp.exp(-jnp.abs(z)))


def _gelu(x):
    return 0.5 * x * (1.0 + lax.erf(x * SQRT_HALF))


def _layernorm(x, g, b):
    mu = jnp.mean(x, axis=-1, keepdims=True)
    xc = x - mu
    return xc * lax.rsqrt(jnp.mean(xc * xc, axis=-1, keepdims=True) + EPS) * g + b


def _dot(a, b):
    return jnp.dot(a, b, preferred_element_type=jnp.float32)


def _dot_nt(a, b):
    return lax.dot_general(a, b, (((1,), (1,)), ((), ())), preferred_element_type=jnp.float32)


def _dot_tn(a, b):
    return lax.dot_general(a, b, (((0,), (0,)), ((), ())), preferred_element_type=jnp.float32)


def _tril(n):
    r = lax.broadcasted_iota(jnp.int32, (n, n), 0)
    c = lax.broadcasted_iota(jnp.int32, (n, n), 1)
    return r >= c


def _prompt_mixer_kernel(proj_ref, pa_ref, wa2_ref, ba_ref, glag_ref, convw_ref, convb_ref, lng_ref, lnb_ref,
                         cmw_ref, cmb_ref, mix_ref, sout_ref, cout_ref, s_ref, zprev_ref,
                         qt_s, kd_s, b_s, oi_s):
    f32, bf16 = jnp.float32, jnp.bfloat16
    t_tile = pl.program_id(1)

    @pl.when(t_tile == 0)
    def _():
        s_ref[...] = jnp.zeros_like(s_ref)
        zprev_ref[...] = jnp.zeros_like(zprev_ref)

    q_scale = GLA_DK ** -0.5
    n_chunks = TT // GLA_CHUNK
    hk = GLA_HEADS * GLA_DK
    head_dk = [slice(h * GLA_DK, (h + 1) * GLA_DK) for h in range(GLA_HEADS)]
    head_dv = [slice(h * GLA_DV, (h + 1) * GLA_DV) for h in range(GLA_HEADS)]

    r = lax.broadcasted_iota(jnp.int32, (GLA_GROUP, GLA_GROUP), 0)
    c = lax.broadcasted_iota(jnp.int32, (GLA_GROUP, GLA_GROUP), 1)
    chunk_shift = GLA_CHUNK.bit_length() - 1
    causal = ((r >> chunk_shift) == (c >> chunk_shift)) & (r >= c)
    causal_ones = causal.astype(bf16)
    group_chunks = GLA_GROUP // GLA_CHUNK

    for grp in range(TT // GLA_GROUP):
        rows = slice(grp * GLA_GROUP, (grp + 1) * GLA_GROUP)
        z = _dot(pa_ref[rows, :], wa2_ref[...]) + ba_ref[...]
        la = _log_sigmoid(z) * (1.0 / GLA_TAU)
        hi = la.astype(bf16)
        rest = la - hi.astype(f32)
        mid = rest.astype(bf16)
        lo = (rest - mid.astype(f32)).astype(bf16)
        cum = _dot(causal_ones, jnp.concatenate([hi, mid, lo], axis=1))
        b = (cum[:, :hk] + cum[:, hk:2 * hk]) + cum[:, 2 * hk:]
        b_s[rows, :] = b
        g = jnp.broadcast_to(b.reshape(group_chunks, GLA_CHUNK, hk)[:, GLA_CHUNK - 1:, :],
                             (group_chunks, GLA_CHUNK, hk)).reshape(GLA_GROUP, hk)
        q = proj_ref[rows, OFF_Q:OFF_K].astype(f32) * q_scale
        k = proj_ref[rows, OFF_K:OFF_V].astype(f32)
        qt = (q * jnp.exp(b)).astype(bf16)
        kt = (k * jnp.exp(-b)).astype(bf16)
        qt_s[rows, :] = qt
        kd_s[rows, :] = (k * jnp.exp(g - b)).astype(bf16)
        for h in range(GLA_HEADS):
            a = jnp.where(causal, _dot_nt(qt[:, head_dk[h]], kt[:, head_dk[h]]), 0.0).astype(bf16)
            oi_s[rows, head_dv[h]] = _dot(a, proj_ref[rows, OFF_V + h * GLA_DV:OFF_V + (h + 1) * GLA_DV])

    def chunk_state(ci, carry):
        rows = pl.ds(pl.multiple_of(ci * GLA_CHUNK, GLA_CHUNK), GLA_CHUNK)
        eg = jnp.exp(b_s[pl.ds(ci * GLA_CHUNK + (GLA_CHUNK - 1), 1), :])
        for h in range(GLA_HEADS):
            dk, dv = head_dk[h], head_dv[h]
            v_h = proj_ref[rows, OFF_V + h * GLA_DV:OFF_V + (h + 1) * GLA_DV]
            s_h = s_ref[h]
            o = oi_s[rows, dv] + _dot(qt_s[rows, dk], s_h.astype(bf16))
            ds = _dot_tn(kd_s[rows, dk], v_h)
            eg_rows = jnp.transpose(jnp.broadcast_to(eg[:, dk], (GLA_DK, GLA_DK)))
            s_ref[h] = jnp.concatenate([eg_rows, eg_rows], axis=1) * s_h + ds
            o = o * lax.rsqrt(jnp.mean(o * o, axis=-1, keepdims=True) + EPS) * glag_ref[:, dv]
            r_h = proj_ref[rows, OFF_R + h * GLA_DV:OFF_R + (h + 1) * GLA_DV].astype(f32)
            mix_ref[rows, dv] = (o * jax.nn.silu(r_h)).astype(bf16)
        return carry

    lax.fori_loop(0, n_chunks, chunk_state, 0, unroll=4)

    w_tril = _tril(CM_CHUNK)
    wm = [jnp.where(w_tril, cmw_ref[h], 0.0).astype(bf16) for h in range(CM_HEADS)]
    row_id = lax.broadcasted_iota(jnp.int32, (CM_CHUNK, D_CONV), 0)

    def cm_block(i, carry):
        rows = pl.ds(pl.multiple_of(i * CM_CHUNK, CM_CHUNK), CM_CHUNK)
        z = proj_ref[rows, OFF_CC:OFF_CH].astype(f32) * proj_ref[rows, OFF_CH:OFF_CU].astype(f32)
        prev = zprev_ref[...]
        p1 = prev[SUBLANES_V7X - 1:SUBLANES_V7X, :]
        p2 = prev[SUBLANES_V7X - 2:SUBLANES_V7X - 1, :]
        z1 = jnp.where(row_id == 0, p1, pltpu.roll(z, 1, 0))
        z2 = jnp.where(row_id == 0, p2, jnp.where(row_id == 1, p1, pltpu.roll(z, 2, 0)))
        conv = convb_ref[...] + convw_ref[0:1, :] * z2
        conv = conv + convw_ref[1:2, :] * z1
        conv = conv + convw_ref[2:3, :] * z
        cb = proj_ref[rows, OFF_CB:OFF_CC].astype(f32)
        mix_ref[rows, MIX_CONV:MIX_CM] = (cb * conv).astype(bf16)
        zprev_ref[...] = z[CM_CHUNK - SUBLANES_V7X:, :]
        cout_ref[0] = z[CM_CHUNK - (CONV_W - 1):, :]

        u = _gelu(proj_ref[rows, OFF_CU:OFF_CV].astype(f32))
        vv = _layernorm(_gelu(proj_ref[rows, OFF_CV:D_MAIN].astype(f32)), lng_ref[...], lnb_ref[...])
        vvb = vv.astype(bf16)
        for h in range(CM_HEADS):
            hd = slice(h * CM_HD, (h + 1) * CM_HD)
            zc = _dot(wm[h], vvb[:, hd]) + cmb_ref[:, hd]
            mix_ref[rows, MIX_CM + h * CM_HD:MIX_CM + (h + 1) * CM_HD] = (u[:, hd] * zc).astype(bf16)
        return carry

    lax.fori_loop(0, TT // CM_CHUNK, cm_block, 0, unroll=4)

    @pl.when(t_tile == pl.num_programs(1) - 1)
    def _():
        sout_ref[0] = s_ref[...]


def prompt_mixer(proj, pa, wa2, ba, glag, convw, convb, lng, lnb, cmw, cmb):
    const = lambda shape: pl.BlockSpec(shape, lambda b, t: (0,) * len(shape))
    tiles = SEQ // TT
    return pl.pallas_call(
        _prompt_mixer_kernel,
        grid=(BATCH, tiles),
        in_specs=[pl.BlockSpec((TT, D_MAIN), lambda b, t: (b * tiles + t, 0)),
                  pl.BlockSpec((TT, A_PAD), lambda b, t: (b * tiles + t, 0)),
                  const((A_PAD, GLA_HEADS * GLA_DK)), const((1, GLA_HEADS * GLA_DK)),
                  const((1, D_GLA)), const((CONV_W, D_CONV)), const((1, D_CONV)),
                  const((1, D_CM)), const((1, D_CM)),
                  const((CM_HEADS, CM_CHUNK, CM_CHUNK)), const((CM_CHUNK, D_CM))],
        out_specs=[pl.BlockSpec((TT, D_MODEL), lambda b, t: (b * tiles + t, 0)),
                   pl.BlockSpec((1, GLA_HEADS, GLA_DK, GLA_DV), lambda b, t: (b, 0, 0, 0)),
                   pl.BlockSpec((1, CONV_W - 1, D_CONV), lambda b, t: (b, 0, 0))],
        out_shape=[jax.ShapeDtypeStruct((N_PROMPT, D_MODEL), jnp.bfloat16),
                   jax.ShapeDtypeStruct((BATCH, GLA_HEADS, GLA_DK, GLA_DV), jnp.float32),
                   jax.ShapeDtypeStruct((BATCH, CONV_W - 1, D_CONV), jnp.float32)],
        scratch_shapes=[pltpu.VMEM((GLA_HEADS, GLA_DK, GLA_DV), jnp.float32),
                        pltpu.VMEM((SUBLANES_V7X, D_CONV), jnp.float32),
                        pltpu.VMEM((TT, GLA_HEADS * GLA_DK), jnp.bfloat16),
                        pltpu.VMEM((TT, GLA_HEADS * GLA_DK), jnp.bfloat16),
                        pltpu.VMEM((TT, GLA_HEADS * GLA_DK), jnp.float32),
                        pltpu.VMEM((TT, D_GLA), jnp.float32)],
        compiler_params=_params(("arbitrary", "arbitrary")),
        name="prompt_mixer",
    )(proj, pa, wa2, ba, glag, convw, convb, lng, lnb, cmw, cmb)


NB = 8
SLOT = SUBLANES_V7X


def _sample_mixer_kernel(proj_ref, pa_ref, sin_ref, _new_state_hbm, cbuf_ref, wa2_ref, ba_ref, glag_ref, convw_ref, convb_ref,
                         lng_ref, lnb_ref, cmw_ref, cmb_ref,
                         mix_ref, sout_ref, cout_ref, vv_ref,
                         qb_s, kb_s, vb_s, ob_s, oi_s, eg_s):
    f32, bf16 = jnp.float32, jnp.bfloat16
    step = pl.program_id(0)
    q_scale = GLA_DK ** -0.5
    trows = [slice(t * DEC_BATCH, (t + 1) * DEC_BATCH) for t in range(DEC_SEQ)]

    @pl.when(step == 0)
    def _():
        qb_s[...] = jnp.zeros_like(qb_s)
        kb_s[...] = jnp.zeros_like(kb_s)
        vb_s[...] = jnp.zeros_like(vb_s)

        cum, b = [], None
        for t in range(DEC_SEQ):
            z = _dot(pa_ref[trows[t], :], wa2_ref[...]) + ba_ref[...]
            la = _log_sigmoid(z) * (1.0 / GLA_TAU)
            b = la if b is None else b + la
            cum.append(b)
        g = cum[-1]
        eg_s[...] = jnp.exp(g)
        qt, kt, vs = [], [], []
        for t in range(DEC_SEQ):
            q = proj_ref[trows[t], OFF_Q:OFF_K].astype(f32) * q_scale
            k = proj_ref[trows[t], OFF_K:OFF_V].astype(f32)
            v = proj_ref[trows[t], OFF_V:OFF_R].astype(f32)
            qt.append(q * jnp.exp(cum[t]))
            kt.append(k * jnp.exp(-cum[t]))
            vs.append(v)
            seq_rows = pl.ds(t, DEC_BATCH, stride=SLOT)
            kd = k * jnp.exp(g - cum[t])
            for j in range(GLA_HEADS):
                qb_s[j, seq_rows, :] = qt[t][:, j * LANES_V7X:(j + 1) * LANES_V7X]
                kb_s[j, seq_rows, :] = kd[:, j * LANES_V7X:(j + 1) * LANES_V7X]
            for j in range(D_GLA // LANES_V7X):
                vb_s[j, seq_rows, :] = v[:, j * LANES_V7X:(j + 1) * LANES_V7X]
        for t in range(DEC_SEQ):
            heads = []
            for h in range(GLA_HEADS):
                dk = slice(h * GLA_DK, (h + 1) * GLA_DK)
                dv = slice(h * GLA_DV, (h + 1) * GLA_DV)
                acc = None
                for s in range(t + 1):
                    a_ts = jnp.sum(qt[t][:, dk] * kt[s][:, dk], axis=-1, keepdims=True)
                    term = a_ts * vs[s][:, dv]
                    acc = term if acc is None else acc + term
                heads.append(acc)
            oi_s[trows[t], :] = jnp.concatenate(heads, axis=1)

        zp = [cbuf_ref[0], cbuf_ref[1]]
        for t in range(DEC_SEQ):
            zp.append(proj_ref[trows[t], OFF_CC:OFF_CH].astype(f32)
                      * proj_ref[trows[t], OFF_CH:OFF_CU].astype(f32))
        for t in range(DEC_SEQ):
            conv = convb_ref[...] + convw_ref[0:1, :] * zp[t]
            conv = conv + convw_ref[1:2, :] * zp[t + 1]
            conv = conv + convw_ref[2:3, :] * zp[t + 2]
            cb = proj_ref[trows[t], OFF_CB:OFF_CC].astype(f32)
            mix_ref[trows[t], MIX_CONV:MIX_CM] = (cb * conv).astype(bf16)
        cout_ref[0] = zp[DEC_SEQ]
        cout_ref[1] = zp[DEC_SEQ + 1]

        vvs = []
        for t in range(DEC_SEQ):
            vv = _layernorm(_gelu(proj_ref[trows[t], OFF_CV:D_MAIN].astype(f32)), lng_ref[...], lnb_ref[...])
            vv_ref[trows[t], :] = vv
            vvs.append(vv)
        for t in range(DEC_SEQ):
            zc = cmb_ref[t:t + 1, :]
            for s in range(t + 1):
                w_ts = cmw_ref[t * DEC_SEQ + s:t * DEC_SEQ + s + 1, :]
                zc = zc + w_ts * vvs[s]
            u = _gelu(proj_ref[trows[t], OFF_CU:OFF_CV].astype(f32))
            mix_ref[trows[t], MIX_CM:] = (u * zc).astype(bf16)

    def seq_body(i, carry):
        seq = step * NB + i
        slot = pl.ds(pl.multiple_of(seq * SLOT, SLOT), SLOT)
        eg_row = eg_s[pl.ds(seq, 1), :]
        for h in range(GLA_HEADS):
            dk = slice(h * GLA_DK, (h + 1) * GLA_DK)
            dv = slice(h * GLA_DV, (h + 1) * GLA_DV)
            s0 = sin_ref[i, h]
            o_seq = _dot(qb_s[h, slot, :].astype(bf16), s0.astype(bf16))
            ob_s[2 * h, slot, :] = o_seq[:, :LANES_V7X]
            ob_s[2 * h + 1, slot, :] = o_seq[:, LANES_V7X:]
            v_seq = jnp.concatenate([vb_s[2 * h, slot, :], vb_s[2 * h + 1, slot, :]], axis=1)
            ds = _dot_tn(kb_s[h, slot, :].astype(bf16), v_seq.astype(bf16))
            eg_rows = jnp.transpose(jnp.broadcast_to(eg_row[:, dk], (GLA_DK, GLA_DK)))
            sout_ref[i, h] = jnp.concatenate([eg_rows, eg_rows], axis=1) * s0 + ds
        return carry

    lax.fori_loop(0, NB, seq_body, 0, unroll=4)

    @pl.when(step == pl.num_programs(0) - 1)
    def _():
        for t in range(DEC_SEQ):
            seq_rows = pl.ds(t, DEC_BATCH, stride=SLOT)
            o_t = oi_s[trows[t], :] + jnp.concatenate(
                [ob_s[j, seq_rows, :] for j in range(D_GLA // LANES_V7X)], axis=1)
            for h in range(GLA_HEADS):
                dv = slice(h * GLA_DV, (h + 1) * GLA_DV)
                o = o_t[:, dv]
                o = o * lax.rsqrt(jnp.mean(o * o, axis=-1, keepdims=True) + EPS) * glag_ref[:, dv]
                r_h = proj_ref[trows[t], OFF_R + h * GLA_DV:OFF_R + (h + 1) * GLA_DV].astype(f32)
                mix_ref[trows[t], dv] = (o * jax.nn.silu(r_h)).astype(bf16)


def sample_mixer(proj, pa, state, new_state, cbuf, wa2, ba, glag, convw, convb, lng, lnb, cmw, cmb, l):
    const = lambda shape: pl.BlockSpec(shape, lambda j: (0,) * len(shape))
    seq_rows = DEC_BATCH * SLOT
    state_block = pl.BlockSpec((None, NB, GLA_HEADS, GLA_DK, GLA_DV), lambda j: (l, j, 0, 0, 0))
    return pl.pallas_call(
        _sample_mixer_kernel,
        grid=(DEC_BATCH // NB,),
        in_specs=[pl.BlockSpec((N_SAMPLE, D_MAIN), lambda j: (N_PROMPT // N_SAMPLE, 0)),
                  pl.BlockSpec((N_SAMPLE, A_PAD), lambda j: (N_PROMPT // N_SAMPLE, 0)),
                  state_block,
                  pl.BlockSpec(memory_space=pl.ANY),
                  const((CONV_W - 1, DEC_BATCH, D_CONV)),
                  const((A_PAD, GLA_HEADS * GLA_DK)), const((1, GLA_HEADS * GLA_DK)),
                  const((1, D_GLA)), const((CONV_W, D_CONV)), const((1, D_CONV)),
                  const((1, D_CM)), const((1, D_CM)),
                  const((DEC_SEQ * DEC_SEQ, D_CM)), const((CM_CHUNK, D_CM))],
        out_specs=[const((N_SAMPLE, D_MODEL)),
                   state_block,
                   const((CONV_W - 1, DEC_BATCH, D_CONV)),
                   const((N_SAMPLE, D_CM))],
        out_shape=[jax.ShapeDtypeStruct((N_SAMPLE, D_MODEL), jnp.bfloat16),
                   jax.ShapeDtypeStruct((DEPTH, DEC_BATCH, GLA_HEADS, GLA_DK, GLA_DV), jnp.float32),
                   jax.ShapeDtypeStruct((CONV_W - 1, DEC_BATCH, D_CONV), jnp.float32),
                   jax.ShapeDtypeStruct((N_SAMPLE, D_CM), jnp.float32)],
        scratch_shapes=[pltpu.VMEM((GLA_HEADS, seq_rows, LANES_V7X), jnp.float32),
                        pltpu.VMEM((GLA_HEADS, seq_rows, LANES_V7X), jnp.float32),
                        pltpu.VMEM((D_GLA // LANES_V7X, seq_rows, LANES_V7X), jnp.float32),
                        pltpu.VMEM((D_GLA // LANES_V7X, seq_rows, LANES_V7X), jnp.float32),
                        pltpu.VMEM((N_SAMPLE, D_GLA), jnp.float32),
                        pltpu.VMEM((DEC_BATCH, GLA_HEADS * GLA_DK), jnp.float32)],
        input_output_aliases={3: 1},
        compiler_params=_params(("arbitrary",)),
        name="sample_mixer",
    )(proj, pa, state, new_state, cbuf, wa2, ba, glag, convw, convb, lng, lnb, cmw, cmb)


def _time_major(x):
    return jnp.swapaxes(x, 0, 1).reshape(N_SAMPLE, x.shape[-1])


def _batch_major(x):
    return jnp.swapaxes(x.reshape(DEC_SEQ, DEC_BATCH, x.shape[-1]), 0, 1)


def _pack_w_in(w_in):
    a0 = OFF_CB
    a1 = a0 + GLA_RANK
    main = jnp.concatenate([w_in[..., :a0], w_in[..., a1:]], axis=-1).astype(jnp.bfloat16)
    gate = jnp.pad(w_in[..., a0:a1], ((0, 0), (0, 0), (0, A_PAD - GLA_RANK))).astype(jnp.bfloat16)
    return main, gate


def kernel(x_prompt, x_sample, state_gla, state_conv, norm1_g, w_in, w_a2, b_a, gla_g, conv_w, conv_b,
           cm_ln_g, cm_ln_b, cm_ws, cm_bs, w_out, norm2_g, w_gate, w_up, w_down, final_g):
    bf = jnp.bfloat16
    w_main, w_gate_rank = _pack_w_in(w_in)
    w_out_b, w_down_b = w_out.astype(bf), w_down.astype(bf)
    wa2_p = jnp.pad(w_a2, ((0, 0), (0, A_PAD - GLA_RANK), (0, 0))).astype(bf)
    cmb_rows = jnp.repeat(jnp.swapaxes(cm_bs, 1, 2), CM_HD, axis=2)
    cmw_small = jnp.repeat(jnp.transpose(cm_ws[:, :, :DEC_SEQ, :DEC_SEQ], (0, 2, 3, 1))
                           .reshape(DEPTH, DEC_SEQ * DEC_SEQ, CM_HEADS), CM_HD, axis=2)
    cbuf_tm = jnp.swapaxes(state_conv, 1, 2)
    row = lambda a: a.reshape(1, -1)

    xp, xs, xs_block = x_prompt.reshape(N_PROMPT, D_MODEL), _time_major(x_sample), 0
    gla_p, conv_p, conv_s, cmv_s = [], [], [], []
    gla_s = jnp.zeros(state_gla.shape, state_gla.dtype)
    for l in range(DEPTH):
        lw = (wa2_p[l], row(b_a[l]), row(gla_g[l]), conv_w[l], row(conv_b[l]), row(cm_ln_g[l]), row(cm_ln_b[l]))
        proj, pa = in_proj(xp, xs, xs_block, row(norm1_g[l]), w_main, w_gate_rank, l)
        mix_p, Sp, cp = prompt_mixer(proj, pa, *lw, cm_ws[l], cmb_rows[l])
        mix_s, gla_s, cs, vs = sample_mixer(proj, pa, state_gla, gla_s, cbuf_tm[l], *lw,
                                            cmw_small[l], cmb_rows[l], l)
        h, hn = out_proj(mix_p, mix_s, w_out_b, xp, xs, xs_block, row(norm2_g[l]), l)
        act = ffn_up(hn, w_gate, w_up, l)
        if l + 1 < DEPTH:
            x = ffn_down(act, w_down_b, h, l)
            xp, xs, xs_block = x, x, N_PROMPT_TILES
        else:
            y_p, y_s = ffn_down_final(act, w_down_b, h, row(final_g), l)
        gla_p.append(Sp); conv_p.append(cp)
        conv_s.append(jnp.swapaxes(cs, 0, 1)); cmv_s.append(_batch_major(vs))
    y_prompt = y_p.reshape(BATCH, SEQ, D_MODEL)
    y_sample = _batch_major(y_s)
    return (y_prompt, y_sample, jnp.stack(gla_p), jnp.stack(conv_p), gla_s,
            jnp.stack(conv_s), jnp.stack(cmv_s))
```

```python
import math

import jax
import jax.numpy as jnp
from jax import lax
from jax.experimental import pallas as pl
from jax.experimental.pallas import tpu as pltpu

D_MODEL = 2048
BATCH = 4
SEQ = 2048
DEPTH = 4
DEC_BATCH = 128
DEC_SEQ = 4
D_GLA = 1024
GLA_HEADS = 4
GLA_DK = 128
GLA_DV = 256
GLA_RANK = 16
GLA_TAU = 16.0
GLA_CHUNK = 64
D_CONV = 512
CONV_W = 3
D_CM = 512
CM_HEADS = 4
CM_HD = 128
CM_CHUNK = 128
D_FF = 5632
EPS = 1e-6
SQRT_HALF = math.sqrt(0.5)

N_PROMPT = BATCH * SEQ
N_SAMPLE = DEC_BATCH * DEC_SEQ
N_TOK = N_PROMPT + N_SAMPLE

OFF_Q, OFF_K, OFF_V, OFF_R = 0, 512, 1024, 2048
OFF_CB, OFF_CC, OFF_CH, OFF_CU, OFF_CV = 3072, 3584, 4096, 4608, 5120
D_MAIN = 5632
A_PAD = 128
MIX_CONV, MIX_CM = D_GLA, D_GLA + D_CONV

LANES_V7X = 128
SUBLANES_V7X = 8
VMEM_LIMIT_V7X = 56 * 1024 * 1024

TM = 1088
TM_UP = 2176
UP_ROWS = 1088
TM_OUT = 512
TN_IN = D_MAIN // 2
TN_FF = 512
TN_DOWN = 512
NORM_ROWS = 32
TT = 1024
GLA_GROUP = 256


def _params(sem):
    return pltpu.CompilerParams(dimension_semantics=sem, vmem_limit_bytes=VMEM_LIMIT_V7X)


def _rms_rows(x_ref, g_ref, o_ref, rows):
    g = g_ref[...]

    def body(c, carry):
        sl = pl.ds(pl.multiple_of(c * NORM_ROWS, NORM_ROWS), NORM_ROWS)
        xf = x_ref[sl, :]
        ms = jnp.mean(xf * xf, axis=-1, keepdims=True)
        o_ref[sl, :] = (xf * lax.rsqrt(ms + EPS) * g).astype(o_ref.dtype)
        return carry

    lax.fori_loop(0, rows // NORM_ROWS, body, 0)


def _rms_rows_unrolled(load_rows, g, o_ref, rows):
    for c in range(rows // NORM_ROWS):
        sl = slice(c * NORM_ROWS, (c + 1) * NORM_ROWS)
        xf = load_rows(sl)
        ms = jnp.mean(xf * xf, axis=-1, keepdims=True)
        o_ref[sl, :] = (xf * lax.rsqrt(ms + EPS) * g).astype(o_ref.dtype)


def _row_tile_specs(sample_block, index):
    prompt = pl.BlockSpec((TM_OUT, D_MODEL),
                          lambda *ids: (jnp.minimum(index(*ids), N_PROMPT_TILES - 1), 0))
    sample = pl.BlockSpec((TM_OUT, D_MODEL), lambda *ids: (sample_block, 0),
                          pipeline_mode=pl.Buffered(1))
    return prompt, sample


N_PROMPT_TILES = N_PROMPT // TM_OUT
N_ROW_TILES = N_TOK // TM_OUT


def _in_proj_kernel(x0_ref, xp_ref, xs_ref, g_ref, w_ref, wa_ref, o_ref, oa_ref, xn_a, xn_b):
    j, i = pl.program_id(0), pl.program_id(1)
    g = g_ref[...]

    @pl.when((i == 0) & (j == 0))
    def _():
        _rms_rows(x0_ref, g_ref, xn_a, TM_OUT)

    next_is_sample = i == N_PROMPT_TILES - 1

    def load_next(sl):
        return jnp.where(next_is_sample, xs_ref[sl, :], xp_ref[sl, :])

    def step(cur, nxt):
        _rms_rows_unrolled(load_next, g, nxt, TM_OUT)
        o_ref[...] = jnp.dot(cur[...], w_ref[...],
                             preferred_element_type=jnp.float32).astype(o_ref.dtype)

        @pl.when(j == 0)
        def _():
            oa_ref[...] = jnp.dot(cur[...], wa_ref[...],
                                  preferred_element_type=jnp.float32).astype(oa_ref.dtype)

    even = (j * N_ROW_TILES + i) % 2 == 0
    pl.when(even)(lambda: step(xn_a, xn_b))
    pl.when(jnp.logical_not(even))(lambda: step(xn_b, xn_a))


def in_proj(xp, xs, xs_block, g, w, wa, l):
    nxt = lambda j, i: (i + 1) % N_ROW_TILES
    xp_spec, xs_spec = _row_tile_specs(xs_block, nxt)
    return pl.pallas_call(
        _in_proj_kernel,
        grid=(D_MAIN // TN_IN, N_ROW_TILES),
        in_specs=[pl.BlockSpec((TM_OUT, D_MODEL), lambda j, i: (0, 0), pipeline_mode=pl.Buffered(1)),
                  xp_spec, xs_spec,
                  pl.BlockSpec((1, D_MODEL), lambda j, i: (0, 0)),
                  pl.BlockSpec((None, D_MODEL, TN_IN), lambda j, i: (l, 0, j)),
                  pl.BlockSpec((None, D_MODEL, A_PAD), lambda j, i: (l, 0, 0))],
        out_specs=[pl.BlockSpec((TM_OUT, TN_IN), lambda j, i: (i, j)),
                   pl.BlockSpec((TM_OUT, A_PAD), lambda j, i: (jnp.where(j == 0, i, N_ROW_TILES - 1), 0))],
        out_shape=[jax.ShapeDtypeStruct((N_TOK, D_MAIN), jnp.bfloat16),
                   jax.ShapeDtypeStruct((N_TOK, A_PAD), jnp.bfloat16)],
        scratch_shapes=[pltpu.VMEM((TM_OUT, D_MODEL), jnp.bfloat16),
                        pltpu.VMEM((TM_OUT, D_MODEL), jnp.bfloat16)],
        compiler_params=_params(("arbitrary", "arbitrary")),
        name="in_proj",
    )(xp, xp, xs, g, w, wa)


def _out_proj_kernel(mixp_ref, mixs_ref, w_ref, xp_ref, xs_ref, g_ref, h_ref, hn_ref, hs_a, hs_b):
    i = pl.program_id(0)
    g = g_ref[...]
    is_sample = i == N_PROMPT_TILES

    def matmul(dst):
        mix = jnp.where(is_sample, mixs_ref[...], mixp_ref[...])
        x = jnp.where(is_sample, xs_ref[...], xp_ref[...])
        dst[...] = x + jnp.dot(mix, w_ref[...], preferred_element_type=jnp.float32)

    def finish(src):
        def load(sl):
            rows = src[sl, :]
            h_ref[sl, :] = rows
            return rows
        _rms_rows_unrolled(load, g, hn_ref, TM_OUT)

    odd = i % 2 == 1
    last = i == N_ROW_TILES

    @pl.when(i == 0)
    def _():
        matmul(hs_a)

    @pl.when(odd & jnp.logical_not(last))
    def _():
        matmul(hs_b)
        finish(hs_a)

    @pl.when(jnp.logical_not(odd) & (i > 0))
    def _():
        matmul(hs_a)
        finish(hs_b)

    @pl.when(last)
    def _():
        finish(hs_a)


def out_proj(mix_p, mix_s, w, xp, xs, xs_block, g, l):
    assert N_ROW_TILES % 2 == 1
    cur = lambda i: jnp.minimum(i, N_ROW_TILES - 1)
    xp_spec, xs_spec = _row_tile_specs(xs_block, cur)
    done = lambda i: (jnp.maximum(i - 1, 0), 0)
    return pl.pallas_call(
        _out_proj_kernel,
        grid=(N_ROW_TILES + 1,),
        in_specs=[pl.BlockSpec((TM_OUT, D_MODEL), lambda i: (jnp.minimum(i, N_PROMPT_TILES - 1), 0)),
                  pl.BlockSpec((N_SAMPLE, D_MODEL), lambda i: (0, 0), pipeline_mode=pl.Buffered(1)),
                  pl.BlockSpec((None, D_MODEL, D_MODEL), lambda i: (l, 0, 0), pipeline_mode=pl.Buffered(1)),
                  xp_spec, xs_spec,
                  pl.BlockSpec((1, D_MODEL), lambda i: (0, 0))],
        out_specs=[pl.BlockSpec((TM_OUT, D_MODEL), done),
                   pl.BlockSpec((TM_OUT, D_MODEL), done)],
        out_shape=[jax.ShapeDtypeStruct((N_TOK, D_MODEL), jnp.float32),
                   jax.ShapeDtypeStruct((N_TOK, D_MODEL), jnp.bfloat16)],
        scratch_shapes=[pltpu.VMEM((TM_OUT, D_MODEL), jnp.float32),
                        pltpu.VMEM((TM_OUT, D_MODEL), jnp.float32)],
        compiler_params=_params(("arbitrary",)),
        name="out_proj",
    )(mix_p, mix_s, w, xp, xs, g)


def _ffn_up_kernel(hn_ref, wg_ref, wu_ref, o_ref, wgb_ref, wub_ref):
    @pl.when(pl.program_id(1) == 0)
    def _():
        wgb_ref[...] = wg_ref[...].astype(jnp.bfloat16)
        wub_ref[...] = wu_ref[...].astype(jnp.bfloat16)

    for s in range(TM_UP // UP_ROWS):
        rows = slice(s * UP_ROWS, (s + 1) * UP_ROWS)
        hn = hn_ref[rows, :]
        gate = jnp.dot(hn, wgb_ref[...], preferred_element_type=jnp.float32)
        up = jnp.dot(hn, wub_ref[...], preferred_element_type=jnp.float32)
        o_ref[rows, :] = (jax.nn.silu(gate) * up).astype(o_ref.dtype)


def ffn_up(hn, wg, wu, l):
    return pl.pallas_call(
        _ffn_up_kernel,
        grid=(D_FF // TN_FF, N_TOK // TM_UP),
        in_specs=[pl.BlockSpec((TM_UP, D_MODEL), lambda j, i: (i, 0)),
                  pl.BlockSpec((None, D_MODEL, TN_FF), lambda j, i: (l, 0, j)),
                  pl.BlockSpec((None, D_MODEL, TN_FF), lambda j, i: (l, 0, j))],
        out_specs=pl.BlockSpec((TM_UP, TN_FF), lambda j, i: (i, j)),
        out_shape=jax.ShapeDtypeStruct((N_TOK, D_FF), jnp.bfloat16),
        scratch_shapes=[pltpu.VMEM((D_MODEL, TN_FF), jnp.bfloat16),
                        pltpu.VMEM((D_MODEL, TN_FF), jnp.bfloat16)],
        compiler_params=_params(("arbitrary", "arbitrary")),
        name="ffn_up",
    )(hn, wg, wu)


def _ffn_down_kernel(a_hbm, w_ref, h_ref, o_ref, a_buf, a_sem):
    i, j = pl.program_id(0), pl.program_id(1)
    slot = i % 2

    def act_copy(tile, dst_slot):
        rows = pl.ds(pl.multiple_of(tile * TM, TM), TM)
        return pltpu.make_async_copy(a_hbm.at[rows, :], a_buf.at[dst_slot], a_sem.at[dst_slot])

    @pl.when(j == 0)
    def _():
        @pl.when(i == 0)
        def _():
            act_copy(0, 0).start()

        act_copy(i, slot).wait()

        @pl.when(i + 1 < pl.num_programs(0))
        def _():
            act_copy(i + 1, 1 - slot).start()

    o_ref[...] = h_ref[...] + jnp.dot(a_buf[slot], w_ref[...],
                                      preferred_element_type=jnp.float32)


def ffn_down(act, w, h, l):
    return pl.pallas_call(
        _ffn_down_kernel,
        grid=(N_TOK // TM, D_MODEL // TN_DOWN),
        in_specs=[pl.BlockSpec(memory_space=pl.ANY),
                  pl.BlockSpec((None, D_FF, TN_DOWN), lambda i, j: (l, 0, j)),
                  pl.BlockSpec((TM, TN_DOWN), lambda i, j: (i, j))],
        out_specs=pl.BlockSpec((TM, TN_DOWN), lambda i, j: (i, j)),
        out_shape=jax.ShapeDtypeStruct((N_TOK, D_MODEL), jnp.float32),
        scratch_shapes=[pltpu.VMEM((2, TM, D_FF), jnp.bfloat16),
                        pltpu.SemaphoreType.DMA((2,))],
        compiler_params=_params(("arbitrary", "arbitrary")),
        name="ffn_down",
    )(act, w, h)


def _ffn_down_final_kernel(a_ref, w_ref, h_ref, g_ref, yp_ref, ys_ref):
    i, j = pl.program_id(0), pl.program_id(1)

    def run(y_ref):
        cols = pl.ds(pl.multiple_of(j * TN_DOWN, TN_DOWN), TN_DOWN)
        y_ref[:, cols] = h_ref[...] + jnp.dot(a_ref[...], w_ref[...],
                                              preferred_element_type=jnp.float32)

        @pl.when(j == pl.num_programs(1) - 1)
        def _():
            _rms_rows(y_ref, g_ref, y_ref, TM_OUT)

    is_prompt = i < N_PROMPT_TILES
    pl.when(is_prompt)(lambda: run(yp_ref))
    pl.when(jnp.logical_not(is_prompt))(lambda: run(ys_ref))


def ffn_down_final(act, w, h, g, l):
    return pl.pallas_call(
        _ffn_down_final_kernel,
        grid=(N_ROW_TILES, D_MODEL // TN_DOWN),
        in_specs=[pl.BlockSpec((TM_OUT, D_FF), lambda i, j: (i, 0)),
                  pl.BlockSpec((None, D_FF, TN_DOWN), lambda i, j: (l, 0, j)),
                  pl.BlockSpec((TM_OUT, TN_DOWN), lambda i, j: (i, j)),
                  pl.BlockSpec((1, D_MODEL), lambda i, j: (0, 0))],
        out_specs=[pl.BlockSpec((TM_OUT, D_MODEL), lambda i, j: (jnp.minimum(i, N_PROMPT_TILES - 1), 0)),
                   pl.BlockSpec((N_SAMPLE, D_MODEL), lambda i, j: (0, 0))],
        out_shape=[jax.ShapeDtypeStruct((N_PROMPT, D_MODEL), jnp.float32),
                   jax.ShapeDtypeStruct((N_SAMPLE, D_MODEL), jnp.float32)],
        compiler_params=_params(("arbitrary", "arbitrary")),
        name="ffn_down_final",
    )(act, w, h, g)


def _log_sigmoid(z):
    return jnp.minimum(z, 0.0) - jnp.log1p(jnp.exp(-jnp.abs(z)))


def _gelu(x):
    return 0.5 * x * (1.0 + lax.erf(x * SQRT_HALF))


def _layernorm(x, g, b):
    mu = jnp.mean(x, axis=-1, keepdims=True)
    xc = x - mu
    return xc * lax.rsqrt(jnp.mean(xc * xc, axis=-1, keepdims=True) + EPS) * g + b


def _dot(a, b):
    return jnp.dot(a, b, preferred_element_type=jnp.float32)


def _dot_nt(a, b):
    return lax.dot_general(a, b, (((1,), (1,)), ((), ())), preferred_element_type=jnp.float32)


def _dot_tn(a, b):
    return lax.dot_general(a, b, (((0,), (0,)), ((), ())), preferred_element_type=jnp.float32)


def _tril(n):
    r = lax.broadcasted_iota(jnp.int32, (n, n), 0)
    c = lax.broadcasted_iota(jnp.int32, (n, n), 1)
    return r >= c


def _prompt_mixer_kernel(proj_ref, pa_ref, wa2_ref, ba_ref, glag_ref, convw_ref, convb_ref, lng_ref, lnb_ref,
                         cmw_ref, cmb_ref, mix_ref, sout_ref, cout_ref, s_ref, zprev_ref,
                         qt_s, kd_s, b_s, oi_s):
    f32, bf16 = jnp.float32, jnp.bfloat16
    t_tile = pl.program_id(1)

    @pl.when(t_tile == 0)
    def _():
        s_ref[...] = jnp.zeros_like(s_ref)
        zprev_ref[...] = jnp.zeros_like(zprev_ref)

    q_scale = GLA_DK ** -0.5
    n_chunks = TT // GLA_CHUNK
    hk = GLA_HEADS * GLA_DK
    head_dk = [slice(h * GLA_DK, (h + 1) * GLA_DK) for h in range(GLA_HEADS)]
    head_dv = [slice(h * GLA_DV, (h + 1) * GLA_DV) for h in range(GLA_HEADS)]

    r = lax.broadcasted_iota(jnp.int32, (GLA_GROUP, GLA_GROUP), 0)
    c = lax.broadcasted_iota(jnp.int32, (GLA_GROUP, GLA_GROUP), 1)
    chunk_shift = GLA_CHUNK.bit_length() - 1
    causal = ((r >> chunk_shift) == (c >> chunk_shift)) & (r >= c)
    causal_ones = causal.astype(bf16)
    group_chunks = GLA_GROUP // GLA_CHUNK

    for grp in range(TT // GLA_GROUP):
        rows = slice(grp * GLA_GROUP, (grp + 1) * GLA_GROUP)
        z = _dot(pa_ref[rows, :], wa2_ref[...]) + ba_ref[...]
        la = _log_sigmoid(z) * (1.0 / GLA_TAU)
        hi = la.astype(bf16)
        rest = la - hi.astype(f32)
        mid = rest.astype(bf16)
        lo = (rest - mid.astype(f32)).astype(bf16)
        cum = _dot(causal_ones, jnp.concatenate([hi, mid, lo], axis=1))
        b = (cum[:, :hk] + cum[:, hk:2 * hk]) + cum[:, 2 * hk:]
        b_s[rows, :] = b
        g = jnp.broadcast_to(b.reshape(group_chunks, GLA_CHUNK, hk)[:, GLA_CHUNK - 1:, :],
                             (group_chunks, GLA_CHUNK, hk)).reshape(GLA_GROUP, hk)
        q = proj_ref[rows, OFF_Q:OFF_K].astype(f32) * q_scale
        k = proj_ref[rows, OFF_K:OFF_V].astype(f32)
        qt = (q * jnp.exp(b)).astype(bf16)
        kt = (k * jnp.exp(-b)).astype(bf16)
        qt_s[rows, :] = qt
        kd_s[rows, :] = (k * jnp.exp(g - b)).astype(bf16)
        for h in range(GLA_HEADS):
            a = jnp.where(causal, _dot_nt(qt[:, head_dk[h]], kt[:, head_dk[h]]), 0.0).astype(bf16)
            oi_s[rows, head_dv[h]] = _dot(a, proj_ref[rows, OFF_V + h * GLA_DV:OFF_V + (h + 1) * GLA_DV])

    def chunk_state(ci, carry):
        rows = pl.ds(pl.multiple_of(ci * GLA_CHUNK, GLA_CHUNK), GLA_CHUNK)
        eg = jnp.exp(b_s[pl.ds(ci * GLA_CHUNK + (GLA_CHUNK - 1), 1), :])
        for h in range(GLA_HEADS):
            dk, dv = head_dk[h], head_dv[h]
            v_h = proj_ref[rows, OFF_V + h * GLA_DV:OFF_V + (h + 1) * GLA_DV]
            s_h = s_ref[h]
            o = oi_s[rows, dv] + _dot(qt_s[rows, dk], s_h.astype(bf16))
            ds = _dot_tn(kd_s[rows, dk], v_h)
            eg_rows = jnp.transpose(jnp.broadcast_to(eg[:, dk], (GLA_DK, GLA_DK)))
            s_ref[h] = jnp.concatenate([eg_rows, eg_rows], axis=1) * s_h + ds
            o = o * lax.rsqrt(jnp.mean(o * o, axis=-1, keepdims=True) + EPS) * glag_ref[:, dv]
            r_h = proj_ref[rows, OFF_R + h * GLA_DV:OFF_R + (h + 1) * GLA_DV].astype(f32)
            mix_ref[rows, dv] = (o * jax.nn.silu(r_h)).astype(bf16)
        return carry

    lax.fori_loop(0, n_chunks, chunk_state, 0, unroll=4)

    w_tril = _tril(CM_CHUNK)
    wm = [jnp.where(w_tril, cmw_ref[h], 0.0).astype(bf16) for h in range(CM_HEADS)]
    row_id = lax.broadcasted_iota(jnp.int32, (CM_CHUNK, D_CONV), 0)

    def cm_block(i, carry):
        rows = pl.ds(pl.multiple_of(i * CM_CHUNK, CM_CHUNK), CM_CHUNK)
        z = proj_ref[rows, OFF_CC:OFF_CH].astype(f32) * proj_ref[rows, OFF_CH:OFF_CU].astype(f32)
        prev = zprev_ref[...]
        p1 = prev[SUBLANES_V7X - 1:SUBLANES_V7X, :]
        p2 = prev[SUBLANES_V7X - 2:SUBLANES_V7X - 1, :]
        z1 = jnp.where(row_id == 0, p1, pltpu.roll(z, 1, 0))
        z2 = jnp.where(row_id == 0, p2, jnp.where(row_id == 1, p1, pltpu.roll(z, 2, 0)))
        conv = convb_ref[...] + convw_ref[0:1, :] * z2
        conv = conv + convw_ref[1:2, :] * z1
        conv = conv + convw_ref[2:3, :] * z
        cb = proj_ref[rows, OFF_CB:OFF_CC].astype(f32)
        mix_ref[rows, MIX_CONV:MIX_CM] = (cb * conv).astype(bf16)
        zprev_ref[...] = z[CM_CHUNK - SUBLANES_V7X:, :]
        cout_ref[0] = z[CM_CHUNK - (CONV_W - 1):, :]

        u = _gelu(proj_ref[rows, OFF_CU:OFF_CV].astype(f32))
        vv = _layernorm(_gelu(proj_ref[rows, OFF_CV:D_MAIN].astype(f32)), lng_ref[...], lnb_ref[...])
        vvb = vv.astype(bf16)
        for h in range(CM_HEADS):
            hd = slice(h * CM_HD, (h + 1) * CM_HD)
            zc = _dot(wm[h], vvb[:, hd]) + cmb_ref[:, hd]
            mix_ref[rows, MIX_CM + h * CM_HD:MIX_CM + (h + 1) * CM_HD] = (u[:, hd] * zc).astype(bf16)
        return carry

    lax.fori_loop(0, TT // CM_CHUNK, cm_block, 0, unroll=4)

    @pl.when(t_tile == pl.num_programs(1) - 1)
    def _():
        sout_ref[0] = s_ref[...]


def prompt_mixer(proj, pa, wa2, ba, glag, convw, convb, lng, lnb, cmw, cmb):
    const = lambda shape: pl.BlockSpec(shape, lambda b, t: (0,) * len(shape))
    tiles = SEQ // TT
    return pl.pallas_call(
        _prompt_mixer_kernel,
        grid=(BATCH, tiles),
        in_specs=[pl.BlockSpec((TT, D_MAIN), lambda b, t: (b * tiles + t, 0)),
                  pl.BlockSpec((TT, A_PAD), lambda b, t: (b * tiles + t, 0)),
                  const((A_PAD, GLA_HEADS * GLA_DK)), const((1, GLA_HEADS * GLA_DK)),
                  const((1, D_GLA)), const((CONV_W, D_CONV)), const((1, D_CONV)),
                  const((1, D_CM)), const((1, D_CM)),
                  const((CM_HEADS, CM_CHUNK, CM_CHUNK)), const((CM_CHUNK, D_CM))],
        out_specs=[pl.BlockSpec((TT, D_MODEL), lambda b, t: (b * tiles + t, 0)),
                   pl.BlockSpec((1, GLA_HEADS, GLA_DK, GLA_DV), lambda b, t: (b, 0, 0, 0)),
                   pl.BlockSpec((1, CONV_W - 1, D_CONV), lambda b, t: (b, 0, 0))],
        out_shape=[jax.ShapeDtypeStruct((N_PROMPT, D_MODEL), jnp.bfloat16),
                   jax.ShapeDtypeStruct((BATCH, GLA_HEADS, GLA_DK, GLA_DV), jnp.float32),
                   jax.ShapeDtypeStruct((BATCH, CONV_W - 1, D_CONV), jnp.float32)],
        scratch_shapes=[pltpu.VMEM((GLA_HEADS, GLA_DK, GLA_DV), jnp.float32),
                        pltpu.VMEM((SUBLANES_V7X, D_CONV), jnp.float32),
                        pltpu.VMEM((TT, GLA_HEADS * GLA_DK), jnp.bfloat16),
                        pltpu.VMEM((TT, GLA_HEADS * GLA_DK), jnp.bfloat16),
                        pltpu.VMEM((TT, GLA_HEADS * GLA_DK), jnp.float32),
                        pltpu.VMEM((TT, D_GLA), jnp.float32)],
        compiler_params=_params(("arbitrary", "arbitrary")),
        name="prompt_mixer",
    )(proj, pa, wa2, ba, glag, convw, convb, lng, lnb, cmw, cmb)


NB = 8
SLOT = SUBLANES_V7X


def _sample_mixer_kernel(proj_ref, pa_ref, sin_ref, _new_state_hbm, cbuf_ref, wa2_ref, ba_ref, glag_ref, convw_ref, convb_ref,
                         lng_ref, lnb_ref, cmw_ref, cmb_ref,
                         mix_ref, sout_ref, cout_ref, vv_ref,
                         qb_s, kb_s, vb_s, ob_s, oi_s, eg_s):
    f32, bf16 = jnp.float32, jnp.bfloat16
    step = pl.program_id(0)
    q_scale = GLA_DK ** -0.5
    trows = [slice(t * DEC_BATCH, (t + 1) * DEC_BATCH) for t in range(DEC_SEQ)]

    @pl.when(step == 0)
    def _():
        qb_s[...] = jnp.zeros_like(qb_s)
        kb_s[...] = jnp.zeros_like(kb_s)
        vb_s[...] = jnp.zeros_like(vb_s)

        cum, b = [], None
        for t in range(DEC_SEQ):
            z = _dot(pa_ref[trows[t], :], wa2_ref[...]) + ba_ref[...]
            la = _log_sigmoid(z) * (1.0 / GLA_TAU)
            b = la if b is None else b + la
            cum.append(b)
        g = cum[-1]
        eg_s[...] = jnp.exp(g)
        qt, kt, vs = [], [], []
        for t in range(DEC_SEQ):
            q = proj_ref[trows[t], OFF_Q:OFF_K].astype(f32) * q_scale
            k = proj_ref[trows[t], OFF_K:OFF_V].astype(f32)
            v = proj_ref[trows[t], OFF_V:OFF_R].astype(f32)
            qt.append(q * jnp.exp(cum[t]))
            kt.append(k * jnp.exp(-cum[t]))
            vs.append(v)
            seq_rows = pl.ds(t, DEC_BATCH, stride=SLOT)
            kd = k * jnp.exp(g - cum[t])
            for j in range(GLA_HEADS):
                qb_s[j, seq_rows, :] = qt[t][:, j * LANES_V7X:(j + 1) * LANES_V7X]
                kb_s[j, seq_rows, :] = kd[:, j * LANES_V7X:(j + 1) * LANES_V7X]
            for j in range(D_GLA // LANES_V7X):
                vb_s[j, seq_rows, :] = v[:, j * LANES_V7X:(j + 1) * LANES_V7X]
        for t in range(DEC_SEQ):
            heads = []
            for h in range(GLA_HEADS):
                dk = slice(h * GLA_DK, (h + 1) * GLA_DK)
                dv = slice(h * GLA_DV, (h + 1) * GLA_DV)
                acc = None
                for s in range(t + 1):
                    a_ts = jnp.sum(qt[t][:, dk] * kt[s][:, dk], axis=-1, keepdims=True)
                    term = a_ts * vs[s][:, dv]
                    acc = term if acc is None else acc + term
                heads.append(acc)
            oi_s[trows[t], :] = jnp.concatenate(heads, axis=1)

        zp = [cbuf_ref[0], cbuf_ref[1]]
        for t in range(DEC_SEQ):
            zp.append(proj_ref[trows[t], OFF_CC:OFF_CH].astype(f32)
                      * proj_ref[trows[t], OFF_CH:OFF_CU].astype(f32))
        for t in range(DEC_SEQ):
            conv = convb_ref[...] + convw_ref[0:1, :] * zp[t]
            conv = conv + convw_ref[1:2, :] * zp[t + 1]
            conv = conv + convw_ref[2:3, :] * zp[t + 2]
            cb = proj_ref[trows[t], OFF_CB:OFF_CC].astype(f32)
            mix_ref[trows[t], MIX_CONV:MIX_CM] = (cb * conv).astype(bf16)
        cout_ref[0] = zp[DEC_SEQ]
        cout_ref[1] = zp[DEC_SEQ + 1]

        vvs = []
        for t in range(DEC_SEQ):
            vv = _layernorm(_gelu(proj_ref[trows[t], OFF_CV:D_MAIN].astype(f32)), lng_ref[...], lnb_ref[...])
            vv_ref[trows[t], :] = vv
            vvs.append(vv)
        for t in range(DEC_SEQ):
            zc = cmb_ref[t:t + 1, :]
            for s in range(t + 1):
                w_ts = cmw_ref[t * DEC_SEQ + s:t * DEC_SEQ + s + 1, :]
                zc = zc + w_ts * vvs[s]
            u = _gelu(proj_ref[trows[t], OFF_CU:OFF_CV].astype(f32))
            mix_ref[trows[t], MIX_CM:] = (u * zc).astype(bf16)

    def seq_body(i, carry):
        seq = step * NB + i
        slot = pl.ds(pl.multiple_of(seq * SLOT, SLOT), SLOT)
        eg_row = eg_s[pl.ds(seq, 1), :]
        for h in range(GLA_HEADS):
            dk = slice(h * GLA_DK, (h + 1) * GLA_DK)
            dv = slice(h * GLA_DV, (h + 1) * GLA_DV)
            s0 = sin_ref[i, h]
            o_seq = _dot(qb_s[h, slot, :].astype(bf16), s0.astype(bf16))
            ob_s[2 * h, slot, :] = o_seq[:, :LANES_V7X]
            ob_s[2 * h + 1, slot, :] = o_seq[:, LANES_V7X:]
            v_seq = jnp.concatenate([vb_s[2 * h, slot, :], vb_s[2 * h + 1, slot, :]], axis=1)
            ds = _dot_tn(kb_s[h, slot, :].astype(bf16), v_seq.astype(bf16))
            eg_rows = jnp.transpose(jnp.broadcast_to(eg_row[:, dk], (GLA_DK, GLA_DK)))
            sout_ref[i, h] = jnp.concatenate([eg_rows, eg_rows], axis=1) * s0 + ds
        return carry

    lax.fori_loop(0, NB, seq_body, 0, unroll=4)

    @pl.when(step == pl.num_programs(0) - 1)
    def _():
        for t in range(DEC_SEQ):
            seq_rows = pl.ds(t, DEC_BATCH, stride=SLOT)
            o_t = oi_s[trows[t], :] + jnp.concatenate(
                [ob_s[j, seq_rows, :] for j in range(D_GLA // LANES_V7X)], axis=1)
            for h in range(GLA_HEADS):
                dv = slice(h * GLA_DV, (h + 1) * GLA_DV)
                o = o_t[:, dv]
                o = o * lax.rsqrt(jnp.mean(o * o, axis=-1, keepdims=True) + EPS) * glag_ref[:, dv]
                r_h = proj_ref[trows[t], OFF_R + h * GLA_DV:OFF_R + (h + 1) * GLA_DV].astype(f32)
                mix_ref[trows[t], dv] = (o * jax.nn.silu(r_h)).astype(bf16)


def sample_mixer(proj, pa, state, new_state, cbuf, wa2, ba, glag, convw, convb, lng, lnb, cmw, cmb, l):
    const = lambda shape: pl.BlockSpec(shape, lambda j: (0,) * len(shape))
    seq_rows = DEC_BATCH * SLOT
    state_block = pl.BlockSpec((None, NB, GLA_HEADS, GLA_DK, GLA_DV), lambda j: (l, j, 0, 0, 0))
    return pl.pallas_call(
        _sample_mixer_kernel,
        grid=(DEC_BATCH // NB,),
        in_specs=[pl.BlockSpec((N_SAMPLE, D_MAIN), lambda j: (N_PROMPT // N_SAMPLE, 0)),
                  pl.BlockSpec((N_SAMPLE, A_PAD), lambda j: (N_PROMPT // N_SAMPLE, 0)),
                  state_block,
                  pl.BlockSpec(memory_space=pl.ANY),
                  const((CONV_W - 1, DEC_BATCH, D_CONV)),
                  const((A_PAD, GLA_HEADS * GLA_DK)), const((1, GLA_HEADS * GLA_DK)),
                  const((1, D_GLA)), const((CONV_W, D_CONV)), const((1, D_CONV)),
                  const((1, D_CM)), const((1, D_CM)),
                  const((DEC_SEQ * DEC_SEQ, D_CM)), const((CM_CHUNK, D_CM))],
        out_specs=[const((N_SAMPLE, D_MODEL)),
                   state_block,
                   const((CONV_W - 1, DEC_BATCH, D_CONV)),
                   const((N_SAMPLE, D_CM))],
        out_shape=[jax.ShapeDtypeStruct((N_SAMPLE, D_MODEL), jnp.bfloat16),
                   jax.ShapeDtypeStruct((DEPTH, DEC_BATCH, GLA_HEADS, GLA_DK, GLA_DV), jnp.float32),
                   jax.ShapeDtypeStruct((CONV_W - 1, DEC_BATCH, D_CONV), jnp.float32),
                   jax.ShapeDtypeStruct((N_SAMPLE, D_CM), jnp.float32)],
        scratch_shapes=[pltpu.VMEM((GLA_HEADS, seq_rows, LANES_V7X), jnp.float32),
                        pltpu.VMEM((GLA_HEADS, seq_rows, LANES_V7X), jnp.float32),
                        pltpu.VMEM((D_GLA // LANES_V7X, seq_rows, LANES_V7X), jnp.float32),
                        pltpu.VMEM((D_GLA // LANES_V7X, seq_rows, LANES_V7X), jnp.float32),
                        pltpu.VMEM((N_SAMPLE, D_GLA), jnp.float32),
                        pltpu.VMEM((DEC_BATCH, GLA_HEADS * GLA_DK), jnp.float32)],
        input_output_aliases={3: 1},
        compiler_params=_params(("arbitrary",)),
        name="sample_mixer",
    )(proj, pa, state, new_state, cbuf, wa2, ba, glag, convw, convb, lng, lnb, cmw, cmb)


def _time_major(x):
    return jnp.swapaxes(x, 0, 1).reshape(N_SAMPLE, x.shape[-1])


def _batch_major(x):
    return jnp.swapaxes(x.reshape(DEC_SEQ, DEC_BATCH, x.shape[-1]), 0, 1)


def _pack_w_in(w_in):
    a0 = OFF_CB
    a1 = a0 + GLA_RANK
    main = jnp.concatenate([w_in[..., :a0], w_in[..., a1:]], axis=-1).astype(jnp.bfloat16)
    gate = jnp.pad(w_in[..., a0:a1], ((0, 0), (0, 0), (0, A_PAD - GLA_RANK))).astype(jnp.bfloat16)
    return main, gate


def kernel(x_prompt, x_sample, state_gla, state_conv, norm1_g, w_in, w_a2, b_a, gla_g, conv_w, conv_b,
           cm_ln_g, cm_ln_b, cm_ws, cm_bs, w_out, norm2_g, w_gate, w_up, w_down, final_g):
    bf = jnp.bfloat16
    w_main, w_gate_rank = _pack_w_in(w_in)
    w_out_b, w_down_b = w_out.astype(bf), w_down.astype(bf)
    wa2_p = jnp.pad(w_a2, ((0, 0), (0, A_PAD - GLA_RANK), (0, 0))).astype(bf)
    cmb_rows = jnp.repeat(jnp.swapaxes(cm_bs, 1, 2), CM_HD, axis=2)
    cmw_small = jnp.repeat(jnp.transpose(cm_ws[:, :, :DEC_SEQ, :DEC_SEQ], (0, 2, 3, 1))
                           .reshape(DEPTH, DEC_SEQ * DEC_SEQ, CM_HEADS), CM_HD, axis=2)
    cbuf_tm = jnp.swapaxes(state_conv, 1, 2)
    row = lambda a: a.reshape(1, -1)

    xp, xs, xs_block = x_prompt.reshape(N_PROMPT, D_MODEL), _time_major(x_sample), 0
    gla_p, conv_p, conv_s, cmv_s = [], [], [], []
    gla_s = jnp.zeros(state_gla.shape, state_gla.dtype)
    for l in range(DEPTH):
        lw = (wa2_p[l], row(b_a[l]), row(gla_g[l]), conv_w[l], row(conv_b[l]), row(cm_ln_g[l]), row(cm_ln_b[l]))
        proj, pa = in_proj(xp, xs, xs_block, row(norm1_g[l]), w_main, w_gate_rank, l)
        mix_p, Sp, cp = prompt_mixer(proj, pa, *lw, cm_ws[l], cmb_rows[l])
        mix_s, gla_s, cs, vs = sample_mixer(proj, pa, state_gla, gla_s, cbuf_tm[l], *lw,
                                            cmw_small[l], cmb_rows[l], l)
        h, hn = out_proj(mix_p, mix_s, w_out_b, xp, xs, xs_block, row(norm2_g[l]), l)
        act = ffn_up(hn, w_gate, w_up, l)
        if l + 1 < DEPTH:
            x = ffn_down(act, w_down_b, h, l)
            xp, xs, xs_block = x, x, N_PROMPT_TILES
        else:
            y_p, y_s = ffn_down_final(act, w_down_b, h, row(final_g), l)
        gla_p.append(Sp); conv_p.append(cp)
        conv_s.append(jnp.swapaxes(cs, 0, 1)); cmv_s.append(_batch_major(vs))
    y_prompt = y_p.reshape(BATCH, SEQ, D_MODEL)
    y_sample = _batch_major(y_s)
    return (y_prompt, y_sample, jnp.stack(gla_p), jnp.stack(conv_p), gla_s,
            jnp.stack(conv_s), jnp.stack(cmv_s))
```

```python
import math

import jax
import jax.numpy as jnp
from jax import lax
from jax.experimental import pallas as pl
from jax.experimental.pallas import tpu as pltpu

D_MODEL = 2048
BATCH = 4
SEQ = 2048
DEPTH = 4
DEC_BATCH = 128
DEC_SEQ = 4
D_GLA = 1024
GLA_HEADS = 4
GLA_DK = 128
GLA_DV = 256
GLA_RANK = 16
GLA_TAU = 16.0
GLA_CHUNK = 64
D_CONV = 512
CONV_W = 3
D_CM = 512
CM_HEADS = 4
CM_HD = 128
CM_CHUNK = 128
D_FF = 5632
EPS = 1e-6
SQRT_HALF = math.sqrt(0.5)

N_PROMPT = BATCH * SEQ
N_SAMPLE = DEC_BATCH * DEC_SEQ
N_TOK = N_PROMPT + N_SAMPLE

OFF_Q, OFF_K, OFF_V, OFF_R = 0, 512, 1024, 2048
OFF_CB, OFF_CC, OFF_CH, OFF_CU, OFF_CV = 3072, 3584, 4096, 4608, 5120
D_MAIN = 5632
A_PAD = 128
MIX_CONV, MIX_CM = D_GLA, D_GLA + D_CONV

LANES_V7X = 128
SUBLANES_V7X = 8
VMEM_LIMIT_V7X = 56 * 1024 * 1024

TM = 1088
TM_UP = 2176
UP_ROWS = 544
TM_OUT = 512
TN_IN = D_MAIN // 2
TN_FF = 512
TN_DOWN = 512
NORM_ROWS = 32
TT = 1024
GLA_GROUP = 256


def _params(sem):
    return pltpu.CompilerParams(dimension_semantics=sem, vmem_limit_bytes=VMEM_LIMIT_V7X)


def _rms_rows(x_ref, g_ref, o_ref, rows):
    g = g_ref[...]

    def body(c, carry):
        sl = pl.ds(pl.multiple_of(c * NORM_ROWS, NORM_ROWS), NORM_ROWS)
        xf = x_ref[sl, :]
        ms = jnp.mean(xf * xf, axis=-1, keepdims=True)
        o_ref[sl, :] = (xf * lax.rsqrt(ms + EPS) * g).astype(o_ref.dtype)
        return carry

    lax.fori_loop(0, rows // NORM_ROWS, body, 0)


def _rms_rows_unrolled(load_rows, g, o_ref, rows):
    for c in range(rows // NORM_ROWS):
        sl = slice(c * NORM_ROWS, (c + 1) * NORM_ROWS)
        xf = load_rows(sl)
        ms = jnp.mean(xf * xf, axis=-1, keepdims=True)
        o_ref[sl, :] = (xf * lax.rsqrt(ms + EPS) * g).astype(o_ref.dtype)


def _row_tile_specs(sample_block, index):
    prompt = pl.BlockSpec((TM_OUT, D_MODEL),
                          lambda *ids: (jnp.minimum(index(*ids), N_PROMPT_TILES - 1), 0))
    sample = pl.BlockSpec((TM_OUT, D_MODEL), lambda *ids: (sample_block, 0),
                          pipeline_mode=pl.Buffered(1))
    return prompt, sample


N_PROMPT_TILES = N_PROMPT // TM_OUT
N_ROW_TILES = N_TOK // TM_OUT


def _in_proj_kernel(x0_ref, xp_ref, xs_ref, g_ref, w_ref, wa_ref, o_ref, oa_ref, xn_a, xn_b):
    j, i = pl.program_id(0), pl.program_id(1)
    g = g_ref[...]

    @pl.when((i == 0) & (j == 0))
    def _():
        _rms_rows(x0_ref, g_ref, xn_a, TM_OUT)

    next_is_sample = i == N_PROMPT_TILES - 1

    def load_next(sl):
        return jnp.where(next_is_sample, xs_ref[sl, :], xp_ref[sl, :])

    def step(cur, nxt):
        _rms_rows_unrolled(load_next, g, nxt, TM_OUT)
        o_ref[...] = jnp.dot(cur[...], w_ref[...],
                             preferred_element_type=jnp.float32).astype(o_ref.dtype)

        @pl.when(j == 0)
        def _():
            oa_ref[...] = jnp.dot(cur[...], wa_ref[...],
                                  preferred_element_type=jnp.float32).astype(oa_ref.dtype)

    even = (j * N_ROW_TILES + i) % 2 == 0
    pl.when(even)(lambda: step(xn_a, xn_b))
    pl.when(jnp.logical_not(even))(lambda: step(xn_b, xn_a))


IN_COLS = D_MAIN // 4


def _in_proj_resident_kernel(x_ref, g_ref, w_ref, wa_ref, o_ref, oa_ref, xn_a, xn_b):
    s = pl.program_id(0)
    g = g_ref[...]

    def step(cur, nxt):
        _rms_rows_unrolled(lambda sl: x_ref[sl, :], g, nxt, TM_OUT)
        xn = cur[...]
        for c in range(D_MAIN // IN_COLS):
            cols = slice(c * IN_COLS, (c + 1) * IN_COLS)
            o_ref[:, cols] = jnp.dot(xn, w_ref[:, cols],
                                     preferred_element_type=jnp.float32).astype(o_ref.dtype)
        oa_ref[...] = jnp.dot(xn, wa_ref[...], preferred_element_type=jnp.float32).astype(oa_ref.dtype)

    @pl.when(s == 0)
    def _():
        _rms_rows(x_ref, g_ref, xn_a, TM_OUT)

    odd = s % 2 == 1
    pl.when(odd)(lambda: step(xn_a, xn_b))
    pl.when(jnp.logical_not(odd) & (s > 0))(lambda: step(xn_b, xn_a))


def in_proj_resident(x, g, w, wa, l):
    done = lambda s: (jnp.maximum(s - 1, 0), 0)
    return pl.pallas_call(
        _in_proj_resident_kernel,
        grid=(N_ROW_TILES + 1,),
        in_specs=[pl.BlockSpec((TM_OUT, D_MODEL), lambda s: (jnp.minimum(s, N_ROW_TILES - 1), 0)),
                  pl.BlockSpec((1, D_MODEL), lambda s: (0, 0)),
                  pl.BlockSpec((None, D_MODEL, D_MAIN), lambda s: (l, 0, 0), pipeline_mode=pl.Buffered(1)),
                  pl.BlockSpec((None, D_MODEL, A_PAD), lambda s: (l, 0, 0), pipeline_mode=pl.Buffered(1))],
        out_specs=[pl.BlockSpec((TM_OUT, D_MAIN), done),
                   pl.BlockSpec((TM_OUT, A_PAD), done)],
        out_shape=[jax.ShapeDtypeStruct((N_TOK, D_MAIN), jnp.bfloat16),
                   jax.ShapeDtypeStruct((N_TOK, A_PAD), jnp.bfloat16)],
        scratch_shapes=[pltpu.VMEM((TM_OUT, D_MODEL), jnp.bfloat16),
                        pltpu.VMEM((TM_OUT, D_MODEL), jnp.bfloat16)],
        compiler_params=_params(("arbitrary",)),
        name="in_proj_resident",
    )(x, g, w, wa)


def in_proj(xp, xs, xs_block, g, w, wa, l):
    nxt = lambda j, i: (i + 1) % N_ROW_TILES
    xp_spec, xs_spec = _row_tile_specs(xs_block, nxt)
    return pl.pallas_call(
        _in_proj_kernel,
        grid=(D_MAIN // TN_IN, N_ROW_TILES),
        in_specs=[pl.BlockSpec((TM_OUT, D_MODEL), lambda j, i: (0, 0), pipeline_mode=pl.Buffered(1)),
                  xp_spec, xs_spec,
                  pl.BlockSpec((1, D_MODEL), lambda j, i: (0, 0)),
                  pl.BlockSpec((None, D_MODEL, TN_IN), lambda j, i: (l, 0, j)),
                  pl.BlockSpec((None, D_MODEL, A_PAD), lambda j, i: (l, 0, 0))],
        out_specs=[pl.BlockSpec((TM_OUT, TN_IN), lambda j, i: (i, j)),
                   pl.BlockSpec((TM_OUT, A_PAD), lambda j, i: (jnp.where(j == 0, i, N_ROW_TILES - 1), 0))],
        out_shape=[jax.ShapeDtypeStruct((N_TOK, D_MAIN), jnp.bfloat16),
                   jax.ShapeDtypeStruct((N_TOK, A_PAD), jnp.bfloat16)],
        scratch_shapes=[pltpu.VMEM((TM_OUT, D_MODEL), jnp.bfloat16),
                        pltpu.VMEM((TM_OUT, D_MODEL), jnp.bfloat16)],
        compiler_params=_params(("arbitrary", "arbitrary")),
        name="in_proj",
    )(xp, xp, xs, g, w, wa)


def _out_proj_kernel(mixp_ref, mixs_ref, w_ref, xp_ref, xs_ref, g_ref, h_ref, hn_ref, hs_a, hs_b):
    i = pl.program_id(0)
    g = g_ref[...]
    is_sample = i == N_PROMPT_TILES

    def matmul(dst):
        mix = jnp.where(is_sample, mixs_ref[...], mixp_ref[...])
        x = jnp.where(is_sample, xs_ref[...], xp_ref[...])
        dst[...] = x + jnp.dot(mix, w_ref[...], preferred_element_type=jnp.float32)

    def finish(src):
        def load(sl):
            rows = src[sl, :]
            h_ref[sl, :] = rows
            return rows
        _rms_rows_unrolled(load, g, hn_ref, TM_OUT)

    odd = i % 2 == 1
    last = i == N_ROW_TILES

    @pl.when(i == 0)
    def _():
        matmul(hs_a)

    @pl.when(odd & jnp.logical_not(last))
    def _():
        matmul(hs_b)
        finish(hs_a)

    @pl.when(jnp.logical_not(odd) & (i > 0))
    def _():
        matmul(hs_a)
        finish(hs_b)

    @pl.when(last)
    def _():
        finish(hs_a)


def out_proj(mix_p, mix_s, w, xp, xs, xs_block, g, l):
    assert N_ROW_TILES % 2 == 1
    cur = lambda i: jnp.minimum(i, N_ROW_TILES - 1)
    xp_spec, xs_spec = _row_tile_specs(xs_block, cur)
    done = lambda i: (jnp.maximum(i - 1, 0), 0)
    return pl.pallas_call(
        _out_proj_kernel,
        grid=(N_ROW_TILES + 1,),
        in_specs=[pl.BlockSpec((TM_OUT, D_MODEL), lambda i: (jnp.minimum(i, N_PROMPT_TILES - 1), 0)),
                  pl.BlockSpec((N_SAMPLE, D_MODEL), lambda i: (0, 0), pipeline_mode=pl.Buffered(1)),
                  pl.BlockSpec((None, D_MODEL, D_MODEL), lambda i: (l, 0, 0), pipeline_mode=pl.Buffered(1)),
                  xp_spec, xs_spec,
                  pl.BlockSpec((1, D_MODEL), lambda i: (0, 0))],
        out_specs=[pl.BlockSpec((TM_OUT, D_MODEL), done),
                   pl.BlockSpec((TM_OUT, D_MODEL), done)],
        out_shape=[jax.ShapeDtypeStruct((N_TOK, D_MODEL), jnp.float32),
                   jax.ShapeDtypeStruct((N_TOK, D_MODEL), jnp.bfloat16)],
        scratch_shapes=[pltpu.VMEM((TM_OUT, D_MODEL), jnp.float32),
                        pltpu.VMEM((TM_OUT, D_MODEL), jnp.float32)],
        compiler_params=_params(("arbitrary",)),
        name="out_proj",
    )(mix_p, mix_s, w, xp, xs, g)


def _ffn_up_kernel(hn_ref, wg_ref, wu_ref, o_ref, wgb_ref, wub_ref):
    @pl.when(pl.program_id(1) == 0)
    def _():
        wgb_ref[...] = wg_ref[...].astype(jnp.bfloat16)
        wub_ref[...] = wu_ref[...].astype(jnp.bfloat16)

    for s in range(TM_UP // UP_ROWS):
        rows = slice(s * UP_ROWS, (s + 1) * UP_ROWS)
        hn = hn_ref[rows, :]
        gate = jnp.dot(hn, wgb_ref[...], preferred_element_type=jnp.float32)
        up = jnp.dot(hn, wub_ref[...], preferred_element_type=jnp.float32)
        o_ref[rows, :] = (jax.nn.silu(gate) * up).astype(o_ref.dtype)


def ffn_up(hn, wg, wu, l):
    return pl.pallas_call(
        _ffn_up_kernel,
        grid=(D_FF // TN_FF, N_TOK // TM_UP),
        in_specs=[pl.BlockSpec((TM_UP, D_MODEL), lambda j, i: (i, 0)),
                  pl.BlockSpec((None, D_MODEL, TN_FF), lambda j, i: (l, 0, j)),
                  pl.BlockSpec((None, D_MODEL, TN_FF), lambda j, i: (l, 0, j))],
        out_specs=pl.BlockSpec((TM_UP, TN_FF), lambda j, i: (i, j)),
        out_shape=jax.ShapeDtypeStruct((N_TOK, D_FF), jnp.bfloat16),
        scratch_shapes=[pltpu.VMEM((D_MODEL, TN_FF), jnp.bfloat16),
                        pltpu.VMEM((D_MODEL, TN_FF), jnp.bfloat16)],
        compiler_params=_params(("arbitrary", "arbitrary")),
        name="ffn_up",
    )(hn, wg, wu)


def _ffn_down_kernel(a_hbm, w_ref, h_ref, o_ref, a_buf, a_sem):
    i, j = pl.program_id(0), pl.program_id(1)
    slot = i % 2

    def act_copy(tile, dst_slot):
        rows = pl.ds(pl.multiple_of(tile * TM, TM), TM)
        return pltpu.make_async_copy(a_hbm.at[rows, :], a_buf.at[dst_slot], a_sem.at[dst_slot])

    @pl.when(j == 0)
    def _():
        @pl.when(i == 0)
        def _():
            act_copy(0, 0).start()

        act_copy(i, slot).wait()

        @pl.when(i + 1 < pl.num_programs(0))
        def _():
            act_copy(i + 1, 1 - slot).start()

    o_ref[...] = h_ref[...] + jnp.dot(a_buf[slot], w_ref[...],
                                      preferred_element_type=jnp.float32)


def ffn_down(act, w, h, l):
    return pl.pallas_call(
        _ffn_down_kernel,
        grid=(N_TOK // TM, D_MODEL // TN_DOWN),
        in_specs=[pl.BlockSpec(memory_space=pl.ANY),
                  pl.BlockSpec((None, D_FF, TN_DOWN), lambda i, j: (l, 0, j)),
                  pl.BlockSpec((TM, TN_DOWN), lambda i, j: (i, j))],
        out_specs=pl.BlockSpec((TM, TN_DOWN), lambda i, j: (i, j)),
        out_shape=jax.ShapeDtypeStruct((N_TOK, D_MODEL), jnp.float32),
        scratch_shapes=[pltpu.VMEM((2, TM, D_FF), jnp.bfloat16),
                        pltpu.SemaphoreType.DMA((2,))],
        compiler_params=_params(("arbitrary", "arbitrary")),
        name="ffn_down",
    )(act, w, h)


def _ffn_down_final_kernel(a_ref, w_ref, h_ref, g_ref, yp_ref, ys_ref):
    i, j = pl.program_id(0), pl.program_id(1)

    def run(y_ref):
        cols = pl.ds(pl.multiple_of(j * TN_DOWN, TN_DOWN), TN_DOWN)
        y_ref[:, cols] = h_ref[...] + jnp.dot(a_ref[...], w_ref[...],
                                              preferred_element_type=jnp.float32)

        @pl.when(j == pl.num_programs(1) - 1)
        def _():
            _rms_rows(y_ref, g_ref, y_ref, TM_OUT)

    is_prompt = i < N_PROMPT_TILES
    pl.when(is_prompt)(lambda: run(yp_ref))
    pl.when(jnp.logical_not(is_prompt))(lambda: run(ys_ref))


def ffn_down_final(act, w, h, g, l):
    return pl.pallas_call(
        _ffn_down_final_kernel,
        grid=(N_ROW_TILES, D_MODEL // TN_DOWN),
        in_specs=[pl.BlockSpec((TM_OUT, D_FF), lambda i, j: (i, 0)),
                  pl.BlockSpec((None, D_FF, TN_DOWN), lambda i, j: (l, 0, j)),
                  pl.BlockSpec((TM_OUT, TN_DOWN), lambda i, j: (i, j)),
                  pl.BlockSpec((1, D_MODEL), lambda i, j: (0, 0))],
        out_specs=[pl.BlockSpec((TM_OUT, D_MODEL), lambda i, j: (jnp.minimum(i, N_PROMPT_TILES - 1), 0)),
                   pl.BlockSpec((N_SAMPLE, D_MODEL), lambda i, j: (0, 0))],
        out_shape=[jax.ShapeDtypeStruct((N_PROMPT, D_MODEL), jnp.float32),
                   jax.ShapeDtypeStruct((N_SAMPLE, D_MODEL), jnp.float32)],
        compiler_params=_params(("arbitrary", "arbitrary")),
        name="ffn_down_final",
    )(act, w, h, g)


def _log_sigmoid(z):
    return jnp.minimum(z, 0.0) - jnp.log1p(jnp.exp(-jnp.abs(z)))


def _gelu(x):
    return 0.5 * x * (1.0 + lax.erf(x * SQRT_HALF))


def _layernorm(x, g, b):
    mu = jnp.mean(x, axis=-1, keepdims=True)
    xc = x - mu
    return xc * lax.rsqrt(jnp.mean(xc * xc, axis=-1, keepdims=True) + EPS) * g + b


def _dot(a, b):
    return jnp.dot(a, b, preferred_element_type=jnp.float32)


def _dot_nt(a, b):
    return lax.dot_general(a, b, (((1,), (1,)), ((), ())), preferred_element_type=jnp.float32)


def _dot_tn(a, b):
    return lax.dot_general(a, b, (((0,), (0,)), ((), ())), preferred_element_type=jnp.float32)


def _tril(n):
    r = lax.broadcasted_iota(jnp.int32, (n, n), 0)
    c = lax.broadcasted_iota(jnp.int32, (n, n), 1)
    return r >= c


def _prompt_mixer_kernel(proj_ref, pa_ref, wa2_ref, ba_ref, glag_ref, convw_ref, convb_ref, lng_ref, lnb_ref,
                         cmw_ref, cmb_ref, mix_ref, sout_ref, cout_ref, s_ref, zprev_ref,
                         qt_s, kd_s, b_s, oi_s):
    f32, bf16 = jnp.float32, jnp.bfloat16
    t_tile = pl.program_id(1)

    @pl.when(t_tile == 0)
    def _():
        s_ref[...] = jnp.zeros_like(s_ref)
        zprev_ref[...] = jnp.zeros_like(zprev_ref)

    q_scale = GLA_DK ** -0.5
    n_chunks = TT // GLA_CHUNK
    hk = GLA_HEADS * GLA_DK
    head_dk = [slice(h * GLA_DK, (h + 1) * GLA_DK) for h in range(GLA_HEADS)]
    head_dv = [slice(h * GLA_DV, (h + 1) * GLA_DV) for h in range(GLA_HEADS)]

    r = lax.broadcasted_iota(jnp.int32, (GLA_GROUP, GLA_GROUP), 0)
    c = lax.broadcasted_iota(jnp.int32, (GLA_GROUP, GLA_GROUP), 1)
    chunk_shift = GLA_CHUNK.bit_length() - 1
    causal = ((r >> chunk_shift) == (c >> chunk_shift)) & (r >= c)
    causal_ones = causal.astype(bf16)
    group_chunks = GLA_GROUP // GLA_CHUNK

    for grp in range(TT // GLA_GROUP):
        rows = slice(grp * GLA_GROUP, (grp + 1) * GLA_GROUP)
        z = _dot(pa_ref[rows, :], wa2_ref[...]) + ba_ref[...]
        la = _log_sigmoid(z) * (1.0 / GLA_TAU)
        hi = la.astype(bf16)
        rest = la - hi.astype(f32)
        mid = rest.astype(bf16)
        lo = (rest - mid.astype(f32)).astype(bf16)
        cum = _dot(causal_ones, jnp.concatenate([hi, mid, lo], axis=1))
        b = (cum[:, :hk] + cum[:, hk:2 * hk]) + cum[:, 2 * hk:]
        b_s[rows, :] = b
        g = jnp.broadcast_to(b.reshape(group_chunks, GLA_CHUNK, hk)[:, GLA_CHUNK - 1:, :],
                             (group_chunks, GLA_CHUNK, hk)).reshape(GLA_GROUP, hk)
        q = proj_ref[rows, OFF_Q:OFF_K].astype(f32) * q_scale
        k = proj_ref[rows, OFF_K:OFF_V].astype(f32)
        qt = (q * jnp.exp(b)).astype(bf16)
        kt = (k * jnp.exp(-b)).astype(bf16)
        qt_s[rows, :] = qt
        kd_s[rows, :] = (k * jnp.exp(g - b)).astype(bf16)
        for h in range(GLA_HEADS):
            a = jnp.where(causal, _dot_nt(qt[:, head_dk[h]], kt[:, head_dk[h]]), 0.0).astype(bf16)
            oi_s[rows, head_dv[h]] = _dot(a, proj_ref[rows, OFF_V + h * GLA_DV:OFF_V + (h + 1) * GLA_DV])

    def chunk_state(ci, carry):
        rows = pl.ds(pl.multiple_of(ci * GLA_CHUNK, GLA_CHUNK), GLA_CHUNK)
        eg = jnp.exp(b_s[pl.ds(ci * GLA_CHUNK + (GLA_CHUNK - 1), 1), :])
        for h in range(GLA_HEADS):
            dk, dv = head_dk[h], head_dv[h]
            v_h = proj_ref[rows, OFF_V + h * GLA_DV:OFF_V + (h + 1) * GLA_DV]
            s_h = s_ref[h]
            o = oi_s[rows, dv] + _dot(qt_s[rows, dk], s_h.astype(bf16))
            ds = _dot_tn(kd_s[rows, dk], v_h)
            eg_rows = jnp.transpose(jnp.broadcast_to(eg[:, dk], (GLA_DK, GLA_DK)))
            s_ref[h] = jnp.concatenate([eg_rows, eg_rows], axis=1) * s_h + ds
            o = o * lax.rsqrt(jnp.mean(o * o, axis=-1, keepdims=True) + EPS) * glag_ref[:, dv]
            r_h = proj_ref[rows, OFF_R + h * GLA_DV:OFF_R + (h + 1) * GLA_DV].astype(f32)
            mix_ref[rows, dv] = (o * jax.nn.silu(r_h)).astype(bf16)
        return carry

    lax.fori_loop(0, n_chunks, chunk_state, 0, unroll=4)

    w_tril = _tril(CM_CHUNK)
    wm = [jnp.where(w_tril, cmw_ref[h], 0.0).astype(bf16) for h in range(CM_HEADS)]
    row_id = lax.broadcasted_iota(jnp.int32, (CM_CHUNK, D_CONV), 0)

    def cm_block(i, carry):
        rows = pl.ds(pl.multiple_of(i * CM_CHUNK, CM_CHUNK), CM_CHUNK)
        z = proj_ref[rows, OFF_CC:OFF_CH].astype(f32) * proj_ref[rows, OFF_CH:OFF_CU].astype(f32)
        prev = zprev_ref[...]
        p1 = prev[SUBLANES_V7X - 1:SUBLANES_V7X, :]
        p2 = prev[SUBLANES_V7X - 2:SUBLANES_V7X - 1, :]
        z1 = jnp.where(row_id == 0, p1, pltpu.roll(z, 1, 0))
        z2 = jnp.where(row_id == 0, p2, jnp.where(row_id == 1, p1, pltpu.roll(z, 2, 0)))
        conv = convb_ref[...] + convw_ref[0:1, :] * z2
        conv = conv + convw_ref[1:2, :] * z1
        conv = conv + convw_ref[2:3, :] * z
        cb = proj_ref[rows, OFF_CB:OFF_CC].astype(f32)
        mix_ref[rows, MIX_CONV:MIX_CM] = (cb * conv).astype(bf16)
        zprev_ref[...] = z[CM_CHUNK - SUBLANES_V7X:, :]
        cout_ref[0] = z[CM_CHUNK - (CONV_W - 1):, :]

        u = _gelu(proj_ref[rows, OFF_CU:OFF_CV].astype(f32))
        vv = _layernorm(_gelu(proj_ref[rows, OFF_CV:D_MAIN].astype(f32)), lng_ref[...], lnb_ref[...])
        vvb = vv.astype(bf16)
        for h in range(CM_HEADS):
            hd = slice(h * CM_HD, (h + 1) * CM_HD)
            zc = _dot(wm[h], vvb[:, hd]) + cmb_ref[:, hd]
            mix_ref[rows, MIX_CM + h * CM_HD:MIX_CM + (h + 1) * CM_HD] = (u[:, hd] * zc).astype(bf16)
        return carry

    lax.fori_loop(0, TT // CM_CHUNK, cm_block, 0, unroll=4)

    @pl.when(t_tile == pl.num_programs(1) - 1)
    def _():
        sout_ref[0] = s_ref[...]


def prompt_mixer(proj, pa, wa2, ba, glag, convw, convb, lng, lnb, cmw, cmb):
    const = lambda shape: pl.BlockSpec(shape, lambda b, t: (0,) * len(shape))
    tiles = SEQ // TT
    return pl.pallas_call(
        _prompt_mixer_kernel,
        grid=(BATCH, tiles),
        in_specs=[pl.BlockSpec((TT, D_MAIN), lambda b, t: (b * tiles + t, 0)),
                  pl.BlockSpec((TT, A_PAD), lambda b, t: (b * tiles + t, 0)),
                  const((A_PAD, GLA_HEADS * GLA_DK)), const((1, GLA_HEADS * GLA_DK)),
                  const((1, D_GLA)), const((CONV_W, D_CONV)), const((1, D_CONV)),
                  const((1, D_CM)), const((1, D_CM)),
                  const((CM_HEADS, CM_CHUNK, CM_CHUNK)), const((CM_CHUNK, D_CM))],
        out_specs=[pl.BlockSpec((TT, D_MODEL), lambda b, t: (b * tiles + t, 0)),
                   pl.BlockSpec((1, GLA_HEADS, GLA_DK, GLA_DV), lambda b, t: (b, 0, 0, 0)),
                   pl.BlockSpec((1, CONV_W - 1, D_CONV), lambda b, t: (b, 0, 0))],
        out_shape=[jax.ShapeDtypeStruct((N_PROMPT, D_MODEL), jnp.bfloat16),
                   jax.ShapeDtypeStruct((BATCH, GLA_HEADS, GLA_DK, GLA_DV), jnp.float32),
                   jax.ShapeDtypeStruct((BATCH, CONV_W - 1, D_CONV), jnp.float32)],
        scratch_shapes=[pltpu.VMEM((GLA_HEADS, GLA_DK, GLA_DV), jnp.float32),
                        pltpu.VMEM((SUBLANES_V7X, D_CONV), jnp.float32),
                        pltpu.VMEM((TT, GLA_HEADS * GLA_DK), jnp.bfloat16),
                        pltpu.VMEM((TT, GLA_HEADS * GLA_DK), jnp.bfloat16),
                        pltpu.VMEM((TT, GLA_HEADS * GLA_DK), jnp.float32),
                        pltpu.VMEM((TT, D_GLA), jnp.float32)],
        compiler_params=_params(("arbitrary", "arbitrary")),
        name="prompt_mixer",
    )(proj, pa, wa2, ba, glag, convw, convb, lng, lnb, cmw, cmb)


NB = 8
SLOT = SUBLANES_V7X


def _sample_mixer_kernel(proj_ref, pa_ref, sin_ref, _new_state_hbm, cbuf_ref, wa2_ref, ba_ref, glag_ref, convw_ref, convb_ref,
                         lng_ref, lnb_ref, cmw_ref, cmb_ref,
                         mix_ref, sout_ref, cout_ref, vv_ref,
                         qb_s, kb_s, vb_s, ob_s, oi_s, eg_s):
    f32, bf16 = jnp.float32, jnp.bfloat16
    step = pl.program_id(0)
    q_scale = GLA_DK ** -0.5
    trows = [slice(t * DEC_BATCH, (t + 1) * DEC_BATCH) for t in range(DEC_SEQ)]

    @pl.when(step == 0)
    def _():
        qb_s[...] = jnp.zeros_like(qb_s)
        kb_s[...] = jnp.zeros_like(kb_s)
        vb_s[...] = jnp.zeros_like(vb_s)

        cum, b = [], None
        for t in range(DEC_SEQ):
            z = _dot(pa_ref[trows[t], :], wa2_ref[...]) + ba_ref[...]
            la = _log_sigmoid(z) * (1.0 / GLA_TAU)
            b = la if b is None else b + la
            cum.append(b)
        g = cum[-1]
        eg_s[...] = jnp.exp(g)
        qt, kt, vs = [], [], []
        for t in range(DEC_SEQ):
            q = proj_ref[trows[t], OFF_Q:OFF_K].astype(f32) * q_scale
            k = proj_ref[trows[t], OFF_K:OFF_V].astype(f32)
            v = proj_ref[trows[t], OFF_V:OFF_R].astype(f32)
            qt.append(q * jnp.exp(cum[t]))
            kt.append(k * jnp.exp(-cum[t]))
            vs.append(v)
            seq_rows = pl.ds(t, DEC_BATCH, stride=SLOT)
            kd = k * jnp.exp(g - cum[t])
            for j in range(GLA_HEADS):
                qb_s[j, seq_rows, :] = qt[t][:, j * LANES_V7X:(j + 1) * LANES_V7X]
                kb_s[j, seq_rows, :] = kd[:, j * LANES_V7X:(j + 1) * LANES_V7X]
            for j in range(D_GLA // LANES_V7X):
                vb_s[j, seq_rows, :] = v[:, j * LANES_V7X:(j + 1) * LANES_V7X]
        for t in range(DEC_SEQ):
            heads = []
            for h in range(GLA_HEADS):
                dk = slice(h * GLA_DK, (h + 1) * GLA_DK)
                dv = slice(h * GLA_DV, (h + 1) * GLA_DV)
                acc = None
                for s in range(t + 1):
                    a_ts = jnp.sum(qt[t][:, dk] * kt[s][:, dk], axis=-1, keepdims=True)
                    term = a_ts * vs[s][:, dv]
                    acc = term if acc is None else acc + term
                heads.append(acc)
            oi_s[trows[t], :] = jnp.concatenate(heads, axis=1)

        zp = [cbuf_ref[0], cbuf_ref[1]]
        for t in range(DEC_SEQ):
            zp.append(proj_ref[trows[t], OFF_CC:OFF_CH].astype(f32)
                      * proj_ref[trows[t], OFF_CH:OFF_CU].astype(f32))
        for t in range(DEC_SEQ):
            conv = convb_ref[...] + convw_ref[0:1, :] * zp[t]
            conv = conv + convw_ref[1:2, :] * zp[t + 1]
            conv = conv + convw_ref[2:3, :] * zp[t + 2]
            cb = proj_ref[trows[t], OFF_CB:OFF_CC].astype(f32)
            mix_ref[trows[t], MIX_CONV:MIX_CM] = (cb * conv).astype(bf16)
        cout_ref[0] = zp[DEC_SEQ]
        cout_ref[1] = zp[DEC_SEQ + 1]

        vvs = []
        for t in range(DEC_SEQ):
            vv = _layernorm(_gelu(proj_ref[trows[t], OFF_CV:D_MAIN].astype(f32)), lng_ref[...], lnb_ref[...])
            vv_ref[trows[t], :] = vv
            vvs.append(vv)
        for t in range(DEC_SEQ):
            zc = cmb_ref[t:t + 1, :]
            for s in range(t + 1):
                w_ts = cmw_ref[t * DEC_SEQ + s:t * DEC_SEQ + s + 1, :]
                zc = zc + w_ts * vvs[s]
            u = _gelu(proj_ref[trows[t], OFF_CU:OFF_CV].astype(f32))
            mix_ref[trows[t], MIX_CM:] = (u * zc).astype(bf16)

    def seq_body(i, carry):
        seq = step * NB + i
        slot = pl.ds(pl.multiple_of(seq * SLOT, SLOT), SLOT)
        eg_row = eg_s[pl.ds(seq, 1), :]
        for h in range(GLA_HEADS):
            dk = slice(h * GLA_DK, (h + 1) * GLA_DK)
            dv = slice(h * GLA_DV, (h + 1) * GLA_DV)
            s0 = sin_ref[i, h]
            o_seq = _dot(qb_s[h, slot, :].astype(bf16), s0.astype(bf16))
            ob_s[2 * h, slot, :] = o_seq[:, :LANES_V7X]
            ob_s[2 * h + 1, slot, :] = o_seq[:, LANES_V7X:]
            v_seq = jnp.concatenate([vb_s[2 * h, slot, :], vb_s[2 * h + 1, slot, :]], axis=1)
            ds = _dot_tn(kb_s[h, slot, :].astype(bf16), v_seq.astype(bf16))
            eg_rows = jnp.transpose(jnp.broadcast_to(eg_row[:, dk], (GLA_DK, GLA_DK)))
            sout_ref[i, h] = jnp.concatenate([eg_rows, eg_rows], axis=1) * s0 + ds
        return carry

    lax.fori_loop(0, NB, seq_body, 0, unroll=4)

    @pl.when(step == pl.num_programs(0) - 1)
    def _():
        for t in range(DEC_SEQ):
            seq_rows = pl.ds(t, DEC_BATCH, stride=SLOT)
            o_t = oi_s[trows[t], :] + jnp.concatenate(
                [ob_s[j, seq_rows, :] for j in range(D_GLA // LANES_V7X)], axis=1)
            for h in range(GLA_HEADS):
                dv = slice(h * GLA_DV, (h + 1) * GLA_DV)
                o = o_t[:, dv]
                o = o * lax.rsqrt(jnp.mean(o * o, axis=-1, keepdims=True) + EPS) * glag_ref[:, dv]
                r_h = proj_ref[trows[t], OFF_R + h * GLA_DV:OFF_R + (h + 1) * GLA_DV].astype(f32)
                mix_ref[trows[t], dv] = (o * jax.nn.silu(r_h)).astype(bf16)


def sample_mixer(proj, pa, state, new_state, cbuf, wa2, ba, glag, convw, convb, lng, lnb, cmw, cmb, l):
    const = lambda shape: pl.BlockSpec(shape, lambda j: (0,) * len(shape))
    seq_rows = DEC_BATCH * SLOT
    state_block = pl.BlockSpec((None, NB, GLA_HEADS, GLA_DK, GLA_DV), lambda j: (l, j, 0, 0, 0))
    return pl.pallas_call(
        _sample_mixer_kernel,
        grid=(DEC_BATCH // NB,),
        in_specs=[pl.BlockSpec((N_SAMPLE, D_MAIN), lambda j: (N_PROMPT // N_SAMPLE, 0)),
                  pl.BlockSpec((N_SAMPLE, A_PAD), lambda j: (N_PROMPT // N_SAMPLE, 0)),
                  state_block,
                  pl.BlockSpec(memory_space=pl.ANY),
                  const((CONV_W - 1, DEC_BATCH, D_CONV)),
                  const((A_PAD, GLA_HEADS * GLA_DK)), const((1, GLA_HEADS * GLA_DK)),
                  const((1, D_GLA)), const((CONV_W, D_CONV)), const((1, D_CONV)),
                  const((1, D_CM)), const((1, D_CM)),
                  const((DEC_SEQ * DEC_SEQ, D_CM)), const((CM_CHUNK, D_CM))],
        out_specs=[const((N_SAMPLE, D_MODEL)),
                   state_block,
                   const((CONV_W - 1, DEC_BATCH, D_CONV)),
                   const((N_SAMPLE, D_CM))],
        out_shape=[jax.ShapeDtypeStruct((N_SAMPLE, D_MODEL), jnp.bfloat16),
                   jax.ShapeDtypeStruct((DEPTH, DEC_BATCH, GLA_HEADS, GLA_DK, GLA_DV), jnp.float32),
                   jax.ShapeDtypeStruct((CONV_W - 1, DEC_BATCH, D_CONV), jnp.float32),
                   jax.ShapeDtypeStruct((N_SAMPLE, D_CM), jnp.float32)],
        scratch_shapes=[pltpu.VMEM((GLA_HEADS, seq_rows, LANES_V7X), jnp.float32),
                        pltpu.VMEM((GLA_HEADS, seq_rows, LANES_V7X), jnp.float32),
                        pltpu.VMEM((D_GLA // LANES_V7X, seq_rows, LANES_V7X), jnp.float32),
                        pltpu.VMEM((D_GLA // LANES_V7X, seq_rows, LANES_V7X), jnp.float32),
                        pltpu.VMEM((N_SAMPLE, D_GLA), jnp.float32),
                        pltpu.VMEM((DEC_BATCH, GLA_HEADS * GLA_DK), jnp.float32)],
        input_output_aliases={3: 1},
        compiler_params=_params(("arbitrary",)),
        name="sample_mixer",
    )(proj, pa, state, new_state, cbuf, wa2, ba, glag, convw, convb, lng, lnb, cmw, cmb)


def _time_major(x):
    return jnp.swapaxes(x, 0, 1).reshape(N_SAMPLE, x.shape[-1])


def _batch_major(x):
    return jnp.swapaxes(x.reshape(DEC_SEQ, DEC_BATCH, x.shape[-1]), 0, 1)


def _pack_w_in(w_in):
    a0 = OFF_CB
    a1 = a0 + GLA_RANK
    main = jnp.concatenate([w_in[..., :a0], w_in[..., a1:]], axis=-1).astype(jnp.bfloat16)
    gate = jnp.pad(w_in[..., a0:a1], ((0, 0), (0, 0), (0, A_PAD - GLA_RANK))).astype(jnp.bfloat16)
    return main, gate


def kernel(x_prompt, x_sample, state_gla, state_conv, norm1_g, w_in, w_a2, b_a, gla_g, conv_w, conv_b,
           cm_ln_g, cm_ln_b, cm_ws, cm_bs, w_out, norm2_g, w_gate, w_up, w_down, final_g):
    bf = jnp.bfloat16
    w_main, w_gate_rank = _pack_w_in(w_in)
    w_out_b, w_down_b = w_out.astype(bf), w_down.astype(bf)
    wa2_p = jnp.pad(w_a2, ((0, 0), (0, A_PAD - GLA_RANK), (0, 0))).astype(bf)
    cmb_rows = jnp.repeat(jnp.swapaxes(cm_bs, 1, 2), CM_HD, axis=2)
    cmw_small = jnp.repeat(jnp.transpose(cm_ws[:, :, :DEC_SEQ, :DEC_SEQ], (0, 2, 3, 1))
                           .reshape(DEPTH, DEC_SEQ * DEC_SEQ, CM_HEADS), CM_HD, axis=2)
    cbuf_tm = jnp.swapaxes(state_conv, 1, 2)
    row = lambda a: a.reshape(1, -1)

    xp, xs, xs_block = x_prompt.reshape(N_PROMPT, D_MODEL), _time_major(x_sample), 0
    gla_p, conv_p, conv_s, cmv_s = [], [], [], []
    gla_s = jnp.zeros(state_gla.shape, state_gla.dtype)
    for l in range(DEPTH):
        lw = (wa2_p[l], row(b_a[l]), row(gla_g[l]), conv_w[l], row(conv_b[l]), row(cm_ln_g[l]), row(cm_ln_b[l]))
        if l == 0:
            proj, pa = in_proj(xp, xs, xs_block, row(norm1_g[l]), w_main, w_gate_rank, l)
        else:
            proj, pa = in_proj_resident(xp, row(norm1_g[l]), w_main, w_gate_rank, l)
        mix_p, Sp, cp = prompt_mixer(proj, pa, *lw, cm_ws[l], cmb_rows[l])
        mix_s, gla_s, cs, vs = sample_mixer(proj, pa, state_gla, gla_s, cbuf_tm[l], *lw,
                                            cmw_small[l], cmb_rows[l], l)
        h, hn = out_proj(mix_p, mix_s, w_out_b, xp, xs, xs_block, row(norm2_g[l]), l)
        act = ffn_up(hn, w_gate, w_up, l)
        if l + 1 < DEPTH:
            x = ffn_down(act, w_down_b, h, l)
            xp, xs, xs_block = x, x, N_PROMPT_TILES
        else:
            y_p, y_s = ffn_down_final(act, w_down_b, h, row(final_g), l)
        gla_p.append(Sp); conv_p.append(cp)
        conv_s.append(jnp.swapaxes(cs, 0, 1)); cmv_s.append(_batch_major(vs))
    y_prompt = y_p.reshape(BATCH, SEQ, D_MODEL)
    y_sample = _batch_major(y_s)
    return (y_prompt, y_sample, jnp.stack(gla_p), jnp.stack(conv_p), gla_s,
            jnp.stack(conv_s), jnp.stack(cmv_s))
```

```python
import math

import jax
import jax.numpy as jnp
from jax import lax
from jax.experimental import pallas as pl
from jax.experimental.pallas import tpu as pltpu

D_MODEL = 2048
BATCH = 4
SEQ = 2048
DEPTH = 4
DEC_BATCH = 128
DEC_SEQ = 4
D_GLA = 1024
GLA_HEADS = 4
GLA_DK = 128
GLA_DV = 256
GLA_RANK = 16
GLA_TAU = 16.0
GLA_CHUNK = 64
D_CONV = 512
CONV_W = 3
D_CM = 512
CM_HEADS = 4
CM_HD = 128
CM_CHUNK = 128
D_FF = 5632
EPS = 1e-6
SQRT_HALF = math.sqrt(0.5)

N_PROMPT = BATCH * SEQ
N_SAMPLE = DEC_BATCH * DEC_SEQ
N_TOK = N_PROMPT + N_SAMPLE

OFF_Q, OFF_K, OFF_V, OFF_R = 0, 512, 1024, 2048
OFF_CB, OFF_CC, OFF_CH, OFF_CU, OFF_CV = 3072, 3584, 4096, 4608, 5120
D_MAIN = 5632
A_PAD = 128
MIX_CONV, MIX_CM = D_GLA, D_GLA + D_CONV

LANES_V7X = 128
SUBLANES_V7X = 8
VMEM_LIMIT_V7X = 56 * 1024 * 1024

TM = 1088
TM_UP = 2176
UP_ROWS = 544
TM_OUT = 512
TN_IN = D_MAIN // 2
TN_FF = 512
TN_DOWN = 512
NORM_ROWS = 32
TT = 1024
GLA_GROUP = 256


def _params(sem):
    return pltpu.CompilerParams(dimension_semantics=sem, vmem_limit_bytes=VMEM_LIMIT_V7X)


def _rms_rows(x_ref, g_ref, o_ref, rows):
    g = g_ref[...]

    def body(c, carry):
        sl = pl.ds(pl.multiple_of(c * NORM_ROWS, NORM_ROWS), NORM_ROWS)
        xf = x_ref[sl, :]
        ms = jnp.mean(xf * xf, axis=-1, keepdims=True)
        o_ref[sl, :] = (xf * lax.rsqrt(ms + EPS) * g).astype(o_ref.dtype)
        return carry

    lax.fori_loop(0, rows // NORM_ROWS, body, 0)


def _rms_rows_unrolled(load_rows, g, o_ref, rows):
    for c in range(rows // NORM_ROWS):
        sl = slice(c * NORM_ROWS, (c + 1) * NORM_ROWS)
        xf = load_rows(sl)
        ms = jnp.mean(xf * xf, axis=-1, keepdims=True)
        o_ref[sl, :] = (xf * lax.rsqrt(ms + EPS) * g).astype(o_ref.dtype)


def _row_tile_specs(sample_block, index):
    prompt = pl.BlockSpec((TM_OUT, D_MODEL),
                          lambda *ids: (jnp.minimum(index(*ids), N_PROMPT_TILES - 1), 0))
    sample = pl.BlockSpec((TM_OUT, D_MODEL), lambda *ids: (sample_block, 0),
                          pipeline_mode=pl.Buffered(1))
    return prompt, sample


N_PROMPT_TILES = N_PROMPT // TM_OUT
N_ROW_TILES = N_TOK // TM_OUT


def _in_proj_kernel(x0_ref, xp_ref, xs_ref, g_ref, w_ref, wa_ref, o_ref, oa_ref, xn_a, xn_b):
    j, i = pl.program_id(0), pl.program_id(1)
    g = g_ref[...]

    @pl.when((i == 0) & (j == 0))
    def _():
        _rms_rows(x0_ref, g_ref, xn_a, TM_OUT)

    next_is_sample = i == N_PROMPT_TILES - 1

    def load_next(sl):
        return jnp.where(next_is_sample, xs_ref[sl, :], xp_ref[sl, :])

    def step(cur, nxt):
        _rms_rows_unrolled(load_next, g, nxt, TM_OUT)
        o_ref[...] = jnp.dot(cur[...], w_ref[...],
                             preferred_element_type=jnp.float32).astype(o_ref.dtype)

        @pl.when(j == 0)
        def _():
            oa_ref[...] = jnp.dot(cur[...], wa_ref[...],
                                  preferred_element_type=jnp.float32).astype(oa_ref.dtype)

    even = (j * N_ROW_TILES + i) % 2 == 0
    pl.when(even)(lambda: step(xn_a, xn_b))
    pl.when(jnp.logical_not(even))(lambda: step(xn_b, xn_a))


MXU_WIDTH_V7X = 256
IN_COLS = 6 * MXU_WIDTH_V7X


def _in_proj_resident_kernel(x_ref, g_ref, w_ref, wa_ref, o_ref, oa_ref, xn_a, xn_b):
    s = pl.program_id(0)
    g = g_ref[...]

    def step(cur, nxt):
        _rms_rows_unrolled(lambda sl: x_ref[sl, :], g, nxt, TM_OUT)
        xn = cur[...]
        for c0 in range(0, D_MAIN, IN_COLS):
            cols = slice(c0, min(c0 + IN_COLS, D_MAIN))
            o_ref[:, cols] = jnp.dot(xn, w_ref[:, cols],
                                     preferred_element_type=jnp.float32).astype(o_ref.dtype)
        oa_ref[...] = jnp.dot(xn, wa_ref[...], preferred_element_type=jnp.float32).astype(oa_ref.dtype)

    @pl.when(s == 0)
    def _():
        _rms_rows(x_ref, g_ref, xn_a, TM_OUT)

    odd = s % 2 == 1
    pl.when(odd)(lambda: step(xn_a, xn_b))
    pl.when(jnp.logical_not(odd) & (s > 0))(lambda: step(xn_b, xn_a))


def in_proj_resident(x, g, w, wa, l):
    done = lambda s: (jnp.maximum(s - 1, 0), 0)
    return pl.pallas_call(
        _in_proj_resident_kernel,
        grid=(N_ROW_TILES + 1,),
        in_specs=[pl.BlockSpec((TM_OUT, D_MODEL), lambda s: (jnp.minimum(s, N_ROW_TILES - 1), 0)),
                  pl.BlockSpec((1, D_MODEL), lambda s: (0, 0)),
                  pl.BlockSpec((None, D_MODEL, D_MAIN), lambda s: (l, 0, 0), pipeline_mode=pl.Buffered(1)),
                  pl.BlockSpec((None, D_MODEL, A_PAD), lambda s: (l, 0, 0), pipeline_mode=pl.Buffered(1))],
        out_specs=[pl.BlockSpec((TM_OUT, D_MAIN), done),
                   pl.BlockSpec((TM_OUT, A_PAD), done)],
        out_shape=[jax.ShapeDtypeStruct((N_TOK, D_MAIN), jnp.bfloat16),
                   jax.ShapeDtypeStruct((N_TOK, A_PAD), jnp.bfloat16)],
        scratch_shapes=[pltpu.VMEM((TM_OUT, D_MODEL), jnp.bfloat16),
                        pltpu.VMEM((TM_OUT, D_MODEL), jnp.bfloat16)],
        compiler_params=_params(("arbitrary",)),
        name="in_proj_resident",
    )(x, g, w, wa)


def in_proj(xp, xs, xs_block, g, w, wa, l):
    nxt = lambda j, i: (i + 1) % N_ROW_TILES
    xp_spec, xs_spec = _row_tile_specs(xs_block, nxt)
    return pl.pallas_call(
        _in_proj_kernel,
        grid=(D_MAIN // TN_IN, N_ROW_TILES),
        in_specs=[pl.BlockSpec((TM_OUT, D_MODEL), lambda j, i: (0, 0), pipeline_mode=pl.Buffered(1)),
                  xp_spec, xs_spec,
                  pl.BlockSpec((1, D_MODEL), lambda j, i: (0, 0)),
                  pl.BlockSpec((None, D_MODEL, TN_IN), lambda j, i: (l, 0, j)),
                  pl.BlockSpec((None, D_MODEL, A_PAD), lambda j, i: (l, 0, 0))],
        out_specs=[pl.BlockSpec((TM_OUT, TN_IN), lambda j, i: (i, j)),
                   pl.BlockSpec((TM_OUT, A_PAD), lambda j, i: (jnp.where(j == 0, i, N_ROW_TILES - 1), 0))],
        out_shape=[jax.ShapeDtypeStruct((N_TOK, D_MAIN), jnp.bfloat16),
                   jax.ShapeDtypeStruct((N_TOK, A_PAD), jnp.bfloat16)],
        scratch_shapes=[pltpu.VMEM((TM_OUT, D_MODEL), jnp.bfloat16),
                        pltpu.VMEM((TM_OUT, D_MODEL), jnp.bfloat16)],
        compiler_params=_params(("arbitrary", "arbitrary")),
        name="in_proj",
    )(xp, xp, xs, g, w, wa)


def _out_proj_kernel(mixp_ref, mixs_ref, w_ref, xp_ref, xs_ref, g_ref, h_ref, hn_ref, hs_a, hs_b):
    i = pl.program_id(0)
    g = g_ref[...]
    is_sample = i == N_PROMPT_TILES

    def matmul(dst):
        mix = jnp.where(is_sample, mixs_ref[...], mixp_ref[...])
        x = jnp.where(is_sample, xs_ref[...], xp_ref[...])
        dst[...] = x + jnp.dot(mix, w_ref[...], preferred_element_type=jnp.float32)

    def finish(src):
        def load(sl):
            rows = src[sl, :]
            h_ref[sl, :] = rows
            return rows
        _rms_rows_unrolled(load, g, hn_ref, TM_OUT)

    odd = i % 2 == 1
    last = i == N_ROW_TILES

    @pl.when(i == 0)
    def _():
        matmul(hs_a)

    @pl.when(odd & jnp.logical_not(last))
    def _():
        matmul(hs_b)
        finish(hs_a)

    @pl.when(jnp.logical_not(odd) & (i > 0))
    def _():
        matmul(hs_a)
        finish(hs_b)

    @pl.when(last)
    def _():
        finish(hs_a)


def out_proj(mix_p, mix_s, w, xp, xs, xs_block, g, l):
    assert N_ROW_TILES % 2 == 1
    cur = lambda i: jnp.minimum(i, N_ROW_TILES - 1)
    xp_spec, xs_spec = _row_tile_specs(xs_block, cur)
    done = lambda i: (jnp.maximum(i - 1, 0), 0)
    return pl.pallas_call(
        _out_proj_kernel,
        grid=(N_ROW_TILES + 1,),
        in_specs=[pl.BlockSpec((TM_OUT, D_MODEL), lambda i: (jnp.minimum(i, N_PROMPT_TILES - 1), 0)),
                  pl.BlockSpec((N_SAMPLE, D_MODEL), lambda i: (0, 0), pipeline_mode=pl.Buffered(1)),
                  pl.BlockSpec((None, D_MODEL, D_MODEL), lambda i: (l, 0, 0), pipeline_mode=pl.Buffered(1)),
                  xp_spec, xs_spec,
                  pl.BlockSpec((1, D_MODEL), lambda i: (0, 0))],
        out_specs=[pl.BlockSpec((TM_OUT, D_MODEL), done),
                   pl.BlockSpec((TM_OUT, D_MODEL), done)],
        out_shape=[jax.ShapeDtypeStruct((N_TOK, D_MODEL), jnp.float32),
                   jax.ShapeDtypeStruct((N_TOK, D_MODEL), jnp.bfloat16)],
        scratch_shapes=[pltpu.VMEM((TM_OUT, D_MODEL), jnp.float32),
                        pltpu.VMEM((TM_OUT, D_MODEL), jnp.float32)],
        compiler_params=_params(("arbitrary",)),
        name="out_proj",
    )(mix_p, mix_s, w, xp, xs, g)


def _ffn_up_kernel(hn_ref, wg_ref, wu_ref, o_ref, wgb_ref, wub_ref):
    @pl.when(pl.program_id(1) == 0)
    def _():
        wgb_ref[...] = wg_ref[...].astype(jnp.bfloat16)
        wub_ref[...] = wu_ref[...].astype(jnp.bfloat16)

    for s in range(TM_UP // UP_ROWS):
        rows = slice(s * UP_ROWS, (s + 1) * UP_ROWS)
        hn = hn_ref[rows, :]
        gate = jnp.dot(hn, wgb_ref[...], preferred_element_type=jnp.float32)
        up = jnp.dot(hn, wub_ref[...], preferred_element_type=jnp.float32)
        o_ref[rows, :] = (jax.nn.silu(gate) * up).astype(o_ref.dtype)


def ffn_up(hn, wg, wu, l):
    return pl.pallas_call(
        _ffn_up_kernel,
        grid=(D_FF // TN_FF, N_TOK // TM_UP),
        in_specs=[pl.BlockSpec((TM_UP, D_MODEL), lambda j, i: (i, 0)),
                  pl.BlockSpec((None, D_MODEL, TN_FF), lambda j, i: (l, 0, j)),
                  pl.BlockSpec((None, D_MODEL, TN_FF), lambda j, i: (l, 0, j))],
        out_specs=pl.BlockSpec((TM_UP, TN_FF), lambda j, i: (i, j)),
        out_shape=jax.ShapeDtypeStruct((N_TOK, D_FF), jnp.bfloat16),
        scratch_shapes=[pltpu.VMEM((D_MODEL, TN_FF), jnp.bfloat16),
                        pltpu.VMEM((D_MODEL, TN_FF), jnp.bfloat16)],
        compiler_params=_params(("arbitrary", "arbitrary")),
        name="ffn_up",
    )(hn, wg, wu)


def _ffn_down_kernel(a_hbm, w_ref, h_ref, o_ref, a_buf, a_sem):
    i, j = pl.program_id(0), pl.program_id(1)
    slot = i % 2

    def act_copy(tile, dst_slot):
        rows = pl.ds(pl.multiple_of(tile * TM, TM), TM)
        return pltpu.make_async_copy(a_hbm.at[rows, :], a_buf.at[dst_slot], a_sem.at[dst_slot])

    @pl.when(j == 0)
    def _():
        @pl.when(i == 0)
        def _():
            act_copy(0, 0).start()

        act_copy(i, slot).wait()

        @pl.when(i + 1 < pl.num_programs(0))
        def _():
            act_copy(i + 1, 1 - slot).start()

    o_ref[...] = h_ref[...] + jnp.dot(a_buf[slot], w_ref[...],
                                      preferred_element_type=jnp.float32)


def ffn_down(act, w, h, l):
    return pl.pallas_call(
        _ffn_down_kernel,
        grid=(N_TOK // TM, D_MODEL // TN_DOWN),
        in_specs=[pl.BlockSpec(memory_space=pl.ANY),
                  pl.BlockSpec((None, D_FF, TN_DOWN), lambda i, j: (l, 0, j)),
                  pl.BlockSpec((TM, TN_DOWN), lambda i, j: (i, j))],
        out_specs=pl.BlockSpec((TM, TN_DOWN), lambda i, j: (i, j)),
        out_shape=jax.ShapeDtypeStruct((N_TOK, D_MODEL), jnp.float32),
        scratch_shapes=[pltpu.VMEM((2, TM, D_FF), jnp.bfloat16),
                        pltpu.SemaphoreType.DMA((2,))],
        compiler_params=_params(("arbitrary", "arbitrary")),
        name="ffn_down",
    )(act, w, h)


def _ffn_down_final_kernel(a_ref, w_ref, h_ref, g_ref, yp_ref, ys_ref):
    i, j = pl.program_id(0), pl.program_id(1)

    def run(y_ref):
        cols = pl.ds(pl.multiple_of(j * TN_DOWN, TN_DOWN), TN_DOWN)
        y_ref[:, cols] = h_ref[...] + jnp.dot(a_ref[...], w_ref[...],
                                              preferred_element_type=jnp.float32)

        @pl.when(j == pl.num_programs(1) - 1)
        def _():
            _rms_rows(y_ref, g_ref, y_ref, TM_OUT)

    is_prompt = i < N_PROMPT_TILES
    pl.when(is_prompt)(lambda: run(yp_ref))
    pl.when(jnp.logical_not(is_prompt))(lambda: run(ys_ref))


def ffn_down_final(act, w, h, g, l):
    return pl.pallas_call(
        _ffn_down_final_kernel,
        grid=(N_ROW_TILES, D_MODEL // TN_DOWN),
        in_specs=[pl.BlockSpec((TM_OUT, D_FF), lambda i, j: (i, 0)),
                  pl.BlockSpec((None, D_FF, TN_DOWN), lambda i, j: (l, 0, j)),
                  pl.BlockSpec((TM_OUT, TN_DOWN), lambda i, j: (i, j)),
                  pl.BlockSpec((1, D_MODEL), lambda i, j: (0, 0))],
        out_specs=[pl.BlockSpec((TM_OUT, D_MODEL), lambda i, j: (jnp.minimum(i, N_PROMPT_TILES - 1), 0)),
                   pl.BlockSpec((N_SAMPLE, D_MODEL), lambda i, j: (0, 0))],
        out_shape=[jax.ShapeDtypeStruct((N_PROMPT, D_MODEL), jnp.float32),
                   jax.ShapeDtypeStruct((N_SAMPLE, D_MODEL), jnp.float32)],
        compiler_params=_params(("arbitrary", "arbitrary")),
        name="ffn_down_final",
    )(act, w, h, g)


def _log_sigmoid(z):
    return jnp.minimum(z, 0.0) - jnp.log1p(jnp.exp(-jnp.abs(z)))


def _gelu(x):
    return 0.5 * x * (1.0 + lax.erf(x * SQRT_HALF))


def _layernorm(x, g, b):
    mu = jnp.mean(x, axis=-1, keepdims=True)
    xc = x - mu
    return xc * lax.rsqrt(jnp.mean(xc * xc, axis=-1, keepdims=True) + EPS) * g + b


def _dot(a, b):
    return jnp.dot(a, b, preferred_element_type=jnp.float32)


def _dot_nt(a, b):
    return lax.dot_general(a, b, (((1,), (1,)), ((), ())), preferred_element_type=jnp.float32)


def _dot_tn(a, b):
    return lax.dot_general(a, b, (((0,), (0,)), ((), ())), preferred_element_type=jnp.float32)


def _tril(n):
    r = lax.broadcasted_iota(jnp.int32, (n, n), 0)
    c = lax.broadcasted_iota(jnp.int32, (n, n), 1)
    return r >= c


def _prompt_mixer_kernel(proj_ref, pa_ref, wa2_ref, ba_ref, glag_ref, convw_ref, convb_ref, lng_ref, lnb_ref,
                         cmw_ref, cmb_ref, mix_ref, sout_ref, cout_ref, s_ref, zprev_ref,
                         qt_s, kd_s, b_s, oi_s):
    f32, bf16 = jnp.float32, jnp.bfloat16
    t_tile = pl.program_id(1)

    @pl.when(t_tile == 0)
    def _():
        s_ref[...] = jnp.zeros_like(s_ref)
        zprev_ref[...] = jnp.zeros_like(zprev_ref)

    q_scale = GLA_DK ** -0.5
    n_chunks = TT // GLA_CHUNK
    hk = GLA_HEADS * GLA_DK
    head_dk = [slice(h * GLA_DK, (h + 1) * GLA_DK) for h in range(GLA_HEADS)]
    head_dv = [slice(h * GLA_DV, (h + 1) * GLA_DV) for h in range(GLA_HEADS)]

    r = lax.broadcasted_iota(jnp.int32, (GLA_GROUP, GLA_GROUP), 0)
    c = lax.broadcasted_iota(jnp.int32, (GLA_GROUP, GLA_GROUP), 1)
    chunk_shift = GLA_CHUNK.bit_length() - 1
    causal = ((r >> chunk_shift) == (c >> chunk_shift)) & (r >= c)
    causal_ones = causal.astype(bf16)
    group_chunks = GLA_GROUP // GLA_CHUNK

    for grp in range(TT // GLA_GROUP):
        rows = slice(grp * GLA_GROUP, (grp + 1) * GLA_GROUP)
        z = _dot(pa_ref[rows, :], wa2_ref[...]) + ba_ref[...]
        la = _log_sigmoid(z) * (1.0 / GLA_TAU)
        hi = la.astype(bf16)
        rest = la - hi.astype(f32)
        mid = rest.astype(bf16)
        lo = (rest - mid.astype(f32)).astype(bf16)
        cum = _dot(causal_ones, jnp.concatenate([hi, mid, lo], axis=1))
        b = (cum[:, :hk] + cum[:, hk:2 * hk]) + cum[:, 2 * hk:]
        b_s[rows, :] = b
        g = jnp.broadcast_to(b.reshape(group_chunks, GLA_CHUNK, hk)[:, GLA_CHUNK - 1:, :],
                             (group_chunks, GLA_CHUNK, hk)).reshape(GLA_GROUP, hk)
        q = proj_ref[rows, OFF_Q:OFF_K].astype(f32) * q_scale
        k = proj_ref[rows, OFF_K:OFF_V].astype(f32)
        qt = (q * jnp.exp(b)).astype(bf16)
        kt = (k * jnp.exp(-b)).astype(bf16)
        qt_s[rows, :] = qt
        kd_s[rows, :] = (k * jnp.exp(g - b)).astype(bf16)
        for h in range(GLA_HEADS):
            a = jnp.where(causal, _dot_nt(qt[:, head_dk[h]], kt[:, head_dk[h]]), 0.0).astype(bf16)
            oi_s[rows, head_dv[h]] = _dot(a, proj_ref[rows, OFF_V + h * GLA_DV:OFF_V + (h + 1) * GLA_DV])

    def chunk_state(ci, carry):
        rows = pl.ds(pl.multiple_of(ci * GLA_CHUNK, GLA_CHUNK), GLA_CHUNK)
        eg = jnp.exp(b_s[pl.ds(ci * GLA_CHUNK + (GLA_CHUNK - 1), 1), :])
        for h in range(GLA_HEADS):
            dk, dv = head_dk[h], head_dv[h]
            v_h = proj_ref[rows, OFF_V + h * GLA_DV:OFF_V + (h + 1) * GLA_DV]
            s_h = s_ref[h]
            o = oi_s[rows, dv] + _dot(qt_s[rows, dk], s_h.astype(bf16))
            ds = _dot_tn(kd_s[rows, dk], v_h)
            eg_rows = jnp.transpose(jnp.broadcast_to(eg[:, dk], (GLA_DK, GLA_DK)))
            s_ref[h] = jnp.concatenate([eg_rows, eg_rows], axis=1) * s_h + ds
            o = o * lax.rsqrt(jnp.mean(o * o, axis=-1, keepdims=True) + EPS) * glag_ref[:, dv]
            r_h = proj_ref[rows, OFF_R + h * GLA_DV:OFF_R + (h + 1) * GLA_DV].astype(f32)
            mix_ref[rows, dv] = (o * jax.nn.silu(r_h)).astype(bf16)
        return carry

    lax.fori_loop(0, n_chunks, chunk_state, 0, unroll=4)

    w_tril = _tril(CM_CHUNK)
    wm = [jnp.where(w_tril, cmw_ref[h], 0.0).astype(bf16) for h in range(CM_HEADS)]
    row_id = lax.broadcasted_iota(jnp.int32, (CM_CHUNK, D_CONV), 0)

    def cm_block(i, carry):
        rows = pl.ds(pl.multiple_of(i * CM_CHUNK, CM_CHUNK), CM_CHUNK)
        z = proj_ref[rows, OFF_CC:OFF_CH].astype(f32) * proj_ref[rows, OFF_CH:OFF_CU].astype(f32)
        prev = zprev_ref[...]
        p1 = prev[SUBLANES_V7X - 1:SUBLANES_V7X, :]
        p2 = prev[SUBLANES_V7X - 2:SUBLANES_V7X - 1, :]
        z1 = jnp.where(row_id == 0, p1, pltpu.roll(z, 1, 0))
        z2 = jnp.where(row_id == 0, p2, jnp.where(row_id == 1, p1, pltpu.roll(z, 2, 0)))
        conv = convb_ref[...] + convw_ref[0:1, :] * z2
        conv = conv + convw_ref[1:2, :] * z1
        conv = conv + convw_ref[2:3, :] * z
        cb = proj_ref[rows, OFF_CB:OFF_CC].astype(f32)
        mix_ref[rows, MIX_CONV:MIX_CM] = (cb * conv).astype(bf16)
        zprev_ref[...] = z[CM_CHUNK - SUBLANES_V7X:, :]
        cout_ref[0] = z[CM_CHUNK - (CONV_W - 1):, :]

        u = _gelu(proj_ref[rows, OFF_CU:OFF_CV].astype(f32))
        vv = _layernorm(_gelu(proj_ref[rows, OFF_CV:D_MAIN].astype(f32)), lng_ref[...], lnb_ref[...])
        vvb = vv.astype(bf16)
        for h in range(CM_HEADS):
            hd = slice(h * CM_HD, (h + 1) * CM_HD)
            zc = _dot(wm[h], vvb[:, hd]) + cmb_ref[:, hd]
            mix_ref[rows, MIX_CM + h * CM_HD:MIX_CM + (h + 1) * CM_HD] = (u[:, hd] * zc).astype(bf16)
        return carry

    lax.fori_loop(0, TT // CM_CHUNK, cm_block, 0, unroll=4)

    @pl.when(t_tile == pl.num_programs(1) - 1)
    def _():
        sout_ref[0] = s_ref[...]


def prompt_mixer(proj, pa, wa2, ba, glag, convw, convb, lng, lnb, cmw, cmb):
    const = lambda shape: pl.BlockSpec(shape, lambda b, t: (0,) * len(shape))
    tiles = SEQ // TT
    return pl.pallas_call(
        _prompt_mixer_kernel,
        grid=(BATCH, tiles),
        in_specs=[pl.BlockSpec((TT, D_MAIN), lambda b, t: (b * tiles + t, 0)),
                  pl.BlockSpec((TT, A_PAD), lambda b, t: (b * tiles + t, 0)),
                  const((A_PAD, GLA_HEADS * GLA_DK)), const((1, GLA_HEADS * GLA_DK)),
                  const((1, D_GLA)), const((CONV_W, D_CONV)), const((1, D_CONV)),
                  const((1, D_CM)), const((1, D_CM)),
                  const((CM_HEADS, CM_CHUNK, CM_CHUNK)), const((CM_CHUNK, D_CM))],
        out_specs=[pl.BlockSpec((TT, D_MODEL), lambda b, t: (b * tiles + t, 0)),
                   pl.BlockSpec((1, GLA_HEADS, GLA_DK, GLA_DV), lambda b, t: (b, 0, 0, 0)),
                   pl.BlockSpec((1, CONV_W - 1, D_CONV), lambda b, t: (b, 0, 0))],
        out_shape=[jax.ShapeDtypeStruct((N_PROMPT, D_MODEL), jnp.bfloat16),
                   jax.ShapeDtypeStruct((BATCH, GLA_HEADS, GLA_DK, GLA_DV), jnp.float32),
                   jax.ShapeDtypeStruct((BATCH, CONV_W - 1, D_CONV), jnp.float32)],
        scratch_shapes=[pltpu.VMEM((GLA_HEADS, GLA_DK, GLA_DV), jnp.float32),
                        pltpu.VMEM((SUBLANES_V7X, D_CONV), jnp.float32),
                        pltpu.VMEM((TT, GLA_HEADS * GLA_DK), jnp.bfloat16),
                        pltpu.VMEM((TT, GLA_HEADS * GLA_DK), jnp.bfloat16),
                        pltpu.VMEM((TT, GLA_HEADS * GLA_DK), jnp.float32),
                        pltpu.VMEM((TT, D_GLA), jnp.float32)],
        compiler_params=_params(("arbitrary", "arbitrary")),
        name="prompt_mixer",
    )(proj, pa, wa2, ba, glag, convw, convb, lng, lnb, cmw, cmb)


NB = 8
SLOT = SUBLANES_V7X


def _sample_mixer_kernel(proj_ref, pa_ref, sin_ref, _new_state_hbm, cbuf_ref, wa2_ref, ba_ref, glag_ref, convw_ref, convb_ref,
                         lng_ref, lnb_ref, cmw_ref, cmb_ref,
                         mix_ref, sout_ref, cout_ref, vv_ref,
                         qb_s, kb_s, vb_s, ob_s, oi_s, eg_s):
    f32, bf16 = jnp.float32, jnp.bfloat16
    step = pl.program_id(0)
    q_scale = GLA_DK ** -0.5
    trows = [slice(t * DEC_BATCH, (t + 1) * DEC_BATCH) for t in range(DEC_SEQ)]

    @pl.when(step == 0)
    def _():
        qb_s[...] = jnp.zeros_like(qb_s)
        kb_s[...] = jnp.zeros_like(kb_s)
        vb_s[...] = jnp.zeros_like(vb_s)

        cum, b = [], None
        for t in range(DEC_SEQ):
            z = _dot(pa_ref[trows[t], :], wa2_ref[...]) + ba_ref[...]
            la = _log_sigmoid(z) * (1.0 / GLA_TAU)
            b = la if b is None else b + la
            cum.append(b)
        g = cum[-1]
        eg_s[...] = jnp.exp(g)
        qt, kt, vs = [], [], []
        for t in range(DEC_SEQ):
            q = proj_ref[trows[t], OFF_Q:OFF_K].astype(f32) * q_scale
            k = proj_ref[trows[t], OFF_K:OFF_V].astype(f32)
            v = proj_ref[trows[t], OFF_V:OFF_R].astype(f32)
            qt.append(q * jnp.exp(cum[t]))
            kt.append(k * jnp.exp(-cum[t]))
            vs.append(v)
            seq_rows = pl.ds(t, DEC_BATCH, stride=SLOT)
            kd = k * jnp.exp(g - cum[t])
            for j in range(GLA_HEADS):
                qb_s[j, seq_rows, :] = qt[t][:, j * LANES_V7X:(j + 1) * LANES_V7X]
                kb_s[j, seq_rows, :] = kd[:, j * LANES_V7X:(j + 1) * LANES_V7X]
            for j in range(D_GLA // LANES_V7X):
                vb_s[j, seq_rows, :] = v[:, j * LANES_V7X:(j + 1) * LANES_V7X]
        for t in range(DEC_SEQ):
            heads = []
            for h in range(GLA_HEADS):
                dk = slice(h * GLA_DK, (h + 1) * GLA_DK)
                dv = slice(h * GLA_DV, (h + 1) * GLA_DV)
                acc = None
                for s in range(t + 1):
                    a_ts = jnp.sum(qt[t][:, dk] * kt[s][:, dk], axis=-1, keepdims=True)
                    term = a_ts * vs[s][:, dv]
                    acc = term if acc is None else acc + term
                heads.append(acc)
            oi_s[trows[t], :] = jnp.concatenate(heads, axis=1)

        zp = [cbuf_ref[0], cbuf_ref[1]]
        for t in range(DEC_SEQ):
            zp.append(proj_ref[trows[t], OFF_CC:OFF_CH].astype(f32)
                      * proj_ref[trows[t], OFF_CH:OFF_CU].astype(f32))
        for t in range(DEC_SEQ):
            conv = convb_ref[...] + convw_ref[0:1, :] * zp[t]
            conv = conv + convw_ref[1:2, :] * zp[t + 1]
            conv = conv + convw_ref[2:3, :] * zp[t + 2]
            cb = proj_ref[trows[t], OFF_CB:OFF_CC].astype(f32)
            mix_ref[trows[t], MIX_CONV:MIX_CM] = (cb * conv).astype(bf16)
        cout_ref[0] = zp[DEC_SEQ]
        cout_ref[1] = zp[DEC_SEQ + 1]

        vvs = []
        for t in range(DEC_SEQ):
            vv = _layernorm(_gelu(proj_ref[trows[t], OFF_CV:D_MAIN].astype(f32)), lng_ref[...], lnb_ref[...])
            vv_ref[trows[t], :] = vv
            vvs.append(vv)
        for t in range(DEC_SEQ):
            zc = cmb_ref[t:t + 1, :]
            for s in range(t + 1):
                w_ts = cmw_ref[t * DEC_SEQ + s:t * DEC_SEQ + s + 1, :]
                zc = zc + w_ts * vvs[s]
            u = _gelu(proj_ref[trows[t], OFF_CU:OFF_CV].astype(f32))
            mix_ref[trows[t], MIX_CM:] = (u * zc).astype(bf16)

    def seq_body(i, carry):
        seq = step * NB + i
        slot = pl.ds(pl.multiple_of(seq * SLOT, SLOT), SLOT)
        eg_row = eg_s[pl.ds(seq, 1), :]
        for h in range(GLA_HEADS):
            dk = slice(h * GLA_DK, (h + 1) * GLA_DK)
            dv = slice(h * GLA_DV, (h + 1) * GLA_DV)
            s0 = sin_ref[i, h]
            o_seq = _dot(qb_s[h, slot, :].astype(bf16), s0.astype(bf16))
            ob_s[2 * h, slot, :] = o_seq[:, :LANES_V7X]
            ob_s[2 * h + 1, slot, :] = o_seq[:, LANES_V7X:]
            v_seq = jnp.concatenate([vb_s[2 * h, slot, :], vb_s[2 * h + 1, slot, :]], axis=1)
            ds = _dot_tn(kb_s[h, slot, :].astype(bf16), v_seq.astype(bf16))
            eg_rows = jnp.transpose(jnp.broadcast_to(eg_row[:, dk], (GLA_DK, GLA_DK)))
            sout_ref[i, h] = jnp.concatenate([eg_rows, eg_rows], axis=1) * s0 + ds
        return carry

    lax.fori_loop(0, NB, seq_body, 0, unroll=4)

    @pl.when(step == pl.num_programs(0) - 1)
    def _():
        for t in range(DEC_SEQ):
            seq_rows = pl.ds(t, DEC_BATCH, stride=SLOT)
            o_t = oi_s[trows[t], :] + jnp.concatenate(
                [ob_s[j, seq_rows, :] for j in range(D_GLA // LANES_V7X)], axis=1)
            for h in range(GLA_HEADS):
                dv = slice(h * GLA_DV, (h + 1) * GLA_DV)
                o = o_t[:, dv]
                o = o * lax.rsqrt(jnp.mean(o * o, axis=-1, keepdims=True) + EPS) * glag_ref[:, dv]
                r_h = proj_ref[trows[t], OFF_R + h * GLA_DV:OFF_R + (h + 1) * GLA_DV].astype(f32)
                mix_ref[trows[t], dv] = (o * jax.nn.silu(r_h)).astype(bf16)


def sample_mixer(proj, pa, state, new_state, cbuf, wa2, ba, glag, convw, convb, lng, lnb, cmw, cmb, l):
    const = lambda shape: pl.BlockSpec(shape, lambda j: (0,) * len(shape))
    seq_rows = DEC_BATCH * SLOT
    state_block = pl.BlockSpec((None, NB, GLA_HEADS, GLA_DK, GLA_DV), lambda j: (l, j, 0, 0, 0))
    return pl.pallas_call(
        _sample_mixer_kernel,
        grid=(DEC_BATCH // NB,),
        in_specs=[pl.BlockSpec((N_SAMPLE, D_MAIN), lambda j: (N_PROMPT // N_SAMPLE, 0)),
                  pl.BlockSpec((N_SAMPLE, A_PAD), lambda j: (N_PROMPT // N_SAMPLE, 0)),
                  state_block,
                  pl.BlockSpec(memory_space=pl.ANY),
                  const((CONV_W - 1, DEC_BATCH, D_CONV)),
                  const((A_PAD, GLA_HEADS * GLA_DK)), const((1, GLA_HEADS * GLA_DK)),
                  const((1, D_GLA)), const((CONV_W, D_CONV)), const((1, D_CONV)),
                  const((1, D_CM)), const((1, D_CM)),
                  const((DEC_SEQ * DEC_SEQ, D_CM)), const((CM_CHUNK, D_CM))],
        out_specs=[const((N_SAMPLE, D_MODEL)),
                   state_block,
                   const((CONV_W - 1, DEC_BATCH, D_CONV)),
                   const((N_SAMPLE, D_CM))],
        out_shape=[jax.ShapeDtypeStruct((N_SAMPLE, D_MODEL), jnp.bfloat16),
                   jax.ShapeDtypeStruct((DEPTH, DEC_BATCH, GLA_HEADS, GLA_DK, GLA_DV), jnp.float32),
                   jax.ShapeDtypeStruct((CONV_W - 1, DEC_BATCH, D_CONV), jnp.float32),
                   jax.ShapeDtypeStruct((N_SAMPLE, D_CM), jnp.float32)],
        scratch_shapes=[pltpu.VMEM((GLA_HEADS, seq_rows, LANES_V7X), jnp.float32),
                        pltpu.VMEM((GLA_HEADS, seq_rows, LANES_V7X), jnp.float32),
                        pltpu.VMEM((D_GLA // LANES_V7X, seq_rows, LANES_V7X), jnp.float32),
                        pltpu.VMEM((D_GLA // LANES_V7X, seq_rows, LANES_V7X), jnp.float32),
                        pltpu.VMEM((N_SAMPLE, D_GLA), jnp.float32),
                        pltpu.VMEM((DEC_BATCH, GLA_HEADS * GLA_DK), jnp.float32)],
        input_output_aliases={3: 1},
        compiler_params=_params(("arbitrary",)),
        name="sample_mixer",
    )(proj, pa, state, new_state, cbuf, wa2, ba, glag, convw, convb, lng, lnb, cmw, cmb)


def _time_major(x):
    return jnp.swapaxes(x, 0, 1).reshape(N_SAMPLE, x.shape[-1])


def _batch_major(x):
    return jnp.swapaxes(x.reshape(DEC_SEQ, DEC_BATCH, x.shape[-1]), 0, 1)


def _pack_w_in(w_in):
    a0 = OFF_CB
    a1 = a0 + GLA_RANK
    main = jnp.concatenate([w_in[..., :a0], w_in[..., a1:]], axis=-1).astype(jnp.bfloat16)
    gate = jnp.pad(w_in[..., a0:a1], ((0, 0), (0, 0), (0, A_PAD - GLA_RANK))).astype(jnp.bfloat16)
    return main, gate


def kernel(x_prompt, x_sample, state_gla, state_conv, norm1_g, w_in, w_a2, b_a, gla_g, conv_w, conv_b,
           cm_ln_g, cm_ln_b, cm_ws, cm_bs, w_out, norm2_g, w_gate, w_up, w_down, final_g):
    bf = jnp.bfloat16
    w_main, w_gate_rank = _pack_w_in(w_in)
    w_out_b, w_down_b = w_out.astype(bf), w_down.astype(bf)
    wa2_p = jnp.pad(w_a2, ((0, 0), (0, A_PAD - GLA_RANK), (0, 0))).astype(bf)
    cmb_rows = jnp.repeat(jnp.swapaxes(cm_bs, 1, 2), CM_HD, axis=2)
    cmw_small = jnp.repeat(jnp.transpose(cm_ws[:, :, :DEC_SEQ, :DEC_SEQ], (0, 2, 3, 1))
                           .reshape(DEPTH, DEC_SEQ * DEC_SEQ, CM_HEADS), CM_HD, axis=2)
    cbuf_tm = jnp.swapaxes(state_conv, 1, 2)
    row = lambda a: a.reshape(1, -1)

    xp, xs, xs_block = x_prompt.reshape(N_PROMPT, D_MODEL), _time_major(x_sample), 0
    gla_p, conv_p, conv_s, cmv_s = [], [], [], []
    gla_s = jnp.zeros(state_gla.shape, state_gla.dtype)
    for l in range(DEPTH):
        lw = (wa2_p[l], row(b_a[l]), row(gla_g[l]), conv_w[l], row(conv_b[l]), row(cm_ln_g[l]), row(cm_ln_b[l]))
        if l == 0:
            proj, pa = in_proj(xp, xs, xs_block, row(norm1_g[l]), w_main, w_gate_rank, l)
        else:
            proj, pa = in_proj_resident(xp, row(norm1_g[l]), w_main, w_gate_rank, l)
        mix_p, Sp, cp = prompt_mixer(proj, pa, *lw, cm_ws[l], cmb_rows[l])
        mix_s, gla_s, cs, vs = sample_mixer(proj, pa, state_gla, gla_s, cbuf_tm[l], *lw,
                                            cmw_small[l], cmb_rows[l], l)
        h, hn = out_proj(mix_p, mix_s, w_out_b, xp, xs, xs_block, row(norm2_g[l]), l)
        act = ffn_up(hn, w_gate, w_up, l)
        if l + 1 < DEPTH:
            x = ffn_down(act, w_down_b, h, l)
            xp, xs, xs_block = x, x, N_PROMPT_TILES
        else:
            y_p, y_s = ffn_down_final(act, w_down_b, h, row(final_g), l)
        gla_p.append(Sp); conv_p.append(cp)
        conv_s.append(jnp.swapaxes(cs, 0, 1)); cmv_s.append(_batch_major(vs))
    y_prompt = y_p.reshape(BATCH, SEQ, D_MODEL)
    y_sample = _batch_major(y_s)
    return (y_prompt, y_sample, jnp.stack(gla_p), jnp.stack(conv_p), gla_s,
            jnp.stack(conv_s), jnp.stack(cmv_s))
```
